```python
import math
import jax, jax.numpy as jnp
from jax import lax
import numpy as np

D_MODEL = 4096
BATCH = 4
SEQ = 4096
DEPTH = 2
DEC_BATCH = 32
DEC_SEQ = 32
PAST_LEN = 4096

CHUNK = 64
N_HEADS = 32
HEAD_DIM = D_MODEL // N_HEADS
N_KV = 8
GROUP = N_HEADS // N_KV
ATTN_WIDTH = N_HEADS * HEAD_DIM
H_IDX = 32
D_IDX = 64
TOPK_MAX = 256
Q_BLOCK = 128
NUM_BUCKETS = 32
MAX_DISTANCE = 128
POOL_WINDOWS = (2, 4, 8, 16)
N_POOL_GROUPS = len(POOL_WINDOWS)
POOL_WIDTH = D_MODEL
GROUP_W = POOL_WIDTH // N_POOL_GROUPS
POOL_HIST = max(POOL_WINDOWS) - 1
EPS = 1e-6
N_ATTN_LAYERS = (DEPTH + 1) // 2
N_POOL_LAYERS = DEPTH // 2

Q_W = ATTN_WIDTH
KV_W = N_KV * HEAD_DIM
QI_W = H_IDX * D_IDX
KI_W = D_IDX
WI_W = H_IDX
ATTN_IN = Q_W + 2 * KV_W + QI_W + KI_W + WI_W + ATTN_WIDTH
ATTN_SPLITS = (Q_W, Q_W + KV_W, Q_W + 2 * KV_W, Q_W + 2 * KV_W + QI_W,
               Q_W + 2 * KV_W + QI_W + KI_W, Q_W + 2 * KV_W + QI_W + KI_W + WI_W)

kernel_name = "hybrid_dsa_pool_streaming_step"

f32 = jnp.float32


def rmsnorm(x, w):
    xf = x.astype(f32)
    y = xf * lax.rsqrt(jnp.mean(xf * xf, axis=-1, keepdims=True) + EPS)
    return (y * w.astype(f32)).astype(x.dtype)


def rel_bucket(rel):
    nb = NUM_BUCKETS // 2
    ret = jnp.where(rel > 0, nb, 0)
    n = jnp.abs(rel)
    max_exact = nb // 2
    nf = jnp.maximum(n, 1).astype(f32)
    large = max_exact + (jnp.log(nf / max_exact) / math.log(MAX_DISTANCE / max_exact)
                         * (nb - max_exact)).astype(jnp.int32)
    large = jnp.minimum(large, nb - 1)
    return ret + jnp.where(n < max_exact, n, large)


def dsa_select_attend(q, qi, wi, q_pos, k_all, v_all, ki_all, k_pos, k_sel, rel_bias):
    B, Q = q.shape[:2]
    dots = jnp.einsum('bqhd,bsd->bqhs', qi, ki_all, preferred_element_type=f32) * (D_IDX ** -0.5)
    score = jnp.einsum('bqhs,bqh->bqs', jax.nn.relu(dots), wi.astype(f32) * (H_IDX ** -0.5))
    adm = (k_pos[None, :] // CHUNK) <= (q_pos[:, None] // CHUNK)
    score = jnp.where(adm[None], score, -jnp.inf)
    _, idx = lax.top_k(score, k_sel)
    take = jax.vmap(lambda a, i: a[i])
    k_g = take(k_all, idx)
    v_g = take(v_all, idx)
    sel_pos = k_pos[idx]
    valid = (sel_pos // CHUNK) <= (q_pos[None, :, None] // CHUNK)
    bias = rel_bias[rel_bucket(sel_pos - q_pos[None, :, None])]
    bias = bias.reshape(B, Q, k_sel, N_KV, GROUP).transpose(0, 1, 3, 4, 2)
    qg = q.reshape(B, Q, N_KV, GROUP, HEAD_DIM)
    logits = jnp.einsum('bqkgd,bqjkd->bqkgj', qg, k_g, preferred_element_type=f32) * (HEAD_DIM ** -0.5)
    logits = logits + bias.astype(f32)
    logits = jnp.where(valid[:, :, None, None, :], logits, -jnp.inf)
    p = jax.nn.softmax(logits, axis=-1)
    out = jnp.einsum('bqkgj,bqjkd->bqkgd', p.astype(v_g.dtype), v_g, preferred_element_type=f32)
    return out.reshape(B, Q, ATTN_WIDTH).astype(q.dtype)


def attn_project(h, w_in):
    B, T, _ = h.shape
    proj = h @ w_in
    q, k, v, qi, ki, wi, gate = jnp.split(proj, ATTN_SPLITS, axis=-1)
    return (q.reshape(B, T, N_HEADS, HEAD_DIM), k.reshape(B, T, N_KV, HEAD_DIM),
            v.reshape(B, T, N_KV, HEAD_DIM), qi.reshape(B, T, H_IDX, D_IDX), ki, wi, gate)


def prompt_attend(q, qi, wi, k, v, ki, rel_bias):
    B, S = q.shape[:2]
    nb = S // Q_BLOCK
    pos = jnp.arange(S)
    k_sel = min(TOPK_MAX, S // 4)

    def to_blocks(a):
        return jnp.moveaxis(a.reshape((B, nb, Q_BLOCK) + a.shape[2:]), 1, 0)

    def blk(args):
        qb, qib, wib, pb = args
        return dsa_select_attend(qb, qib, wib, pb, k, v, ki, pos, k_sel, rel_bias)

    out = lax.map(blk, (to_blocks(q), to_blocks(qi), to_blocks(wi), pos.reshape(nb, Q_BLOCK)))
    return jnp.moveaxis(out, 0, 1).reshape(B, S, ATTN_WIDTH)


def sample_attend(q, qi, wi, k, v, ki, ck, cv, cki, rel_bias):
    T = q.shape[1]
    P = ck.shape[1]
    k_all = jnp.concatenate([ck.astype(k.dtype), k], axis=1)
    v_all = jnp.concatenate([cv.astype(v.dtype), v], axis=1)
    ki_all = jnp.concatenate([cki.astype(ki.dtype), ki], axis=1)
    k_pos = jnp.arange(P + T)
    q_pos = P + jnp.arange(T)
    k_sel = min(TOPK_MAX, (P + T) // 4)
    return dsa_select_attend(q, qi, wi, q_pos, k_all, v_all, ki_all, k_pos, k_sel, rel_bias)


def pool_mix(u, hist, start):
    B, T, E = u.shape
    P = POOL_HIST
    ext = jnp.concatenate([jnp.zeros((B, 1, E), f32), hist.astype(f32), u.astype(f32)], axis=1)
    c = jnp.cumsum(ext, axis=1)
    ends = c[:, P + 1:P + 1 + T]
    pos = start + jnp.arange(T)
    outs = []
    for g, w in enumerate(POOL_WINDOWS):
        lo, hi = g * GROUP_W, (g + 1) * GROUP_W
        s = ends[..., lo:hi] - c[:, P + 1 - w:P + 1 - w + T, lo:hi]
        cnt = jnp.minimum(pos + 1, w).astype(f32)
        outs.append(s / cnt[None, :, None])
    mean = jnp.concatenate(outs, axis=-1)
    return (mean - u.astype(f32)).astype(u.dtype)


def pool_layer(h, hist, start, w_in, group_w, scale, w_out):
    B, T, _ = h.shape
    proj = h @ w_in
    u, gate = proj[..., :POOL_WIDTH], proj[..., POOL_WIDTH:]
    m = pool_mix(u, hist, start)
    z = jnp.einsum('btgc,gcd->btgd', m.reshape(B, T, N_POOL_GROUPS, GROUP_W), group_w).reshape(B, T, POOL_WIDTH)
    z = z * scale
    out = (z * jax.nn.silu(gate)) @ w_out
    new_hist = jnp.concatenate([hist.astype(u.dtype), u], axis=1)[:, -POOL_HIST:]
    return out, new_hist


def setup_inputs(seed: int = 0) -> dict:
    key = jax.random.key(seed)
    ks = jax.random.split(key, 16)
    NA, NP = N_ATTN_LAYERS, N_POOL_LAYERS
    nrm = jax.random.normal
    return {
        "x_prompt": nrm(ks[0], (BATCH, SEQ, D_MODEL), f32),
        "x_sample": nrm(ks[1], (DEC_BATCH, DEC_SEQ, D_MODEL), f32),
        "cache_k": nrm(ks[2], (NA, DEC_BATCH, PAST_LEN, N_KV, HEAD_DIM), f32),
        "cache_v": nrm(ks[3], (NA, DEC_BATCH, PAST_LEN, N_KV, HEAD_DIM), f32),
        "cache_kidx": nrm(ks[4], (NA, DEC_BATCH, PAST_LEN, D_IDX), f32),
        "state_pool": nrm(ks[5], (NP, DEC_BATCH, POOL_HIST, POOL_WIDTH), f32),
        "norm_w": 1.0 + 0.02 * nrm(ks[6], (DEPTH, D_MODEL), f32),
        "final_norm_w": 1.0 + 0.02 * nrm(ks[7], (D_MODEL,), f32),
        "attn_w_in": nrm(ks[8], (NA, D_MODEL, ATTN_IN), f32) * D_MODEL ** -0.5,
        "attn_w_out": nrm(ks[9], (NA, ATTN_WIDTH, D_MODEL), f32) * ATTN_WIDTH ** -0.5,
        "rel_bias": 0.2 * nrm(ks[10], (NUM_BUCKETS, N_HEADS), f32),
        "pool_w_in": nrm(ks[11], (NP, D_MODEL, 2 * POOL_WIDTH), f32) * D_MODEL ** -0.5,
        "pool_group_w": nrm(ks[12], (NP, N_POOL_GROUPS, GROUP_W, GROUP_W), f32) * GROUP_W ** -0.5,
        "pool_scale": 1.0 + 0.05 * nrm(ks[13], (NP, POOL_WIDTH), f32),
        "pool_w_out": nrm(ks[14], (NP, POOL_WIDTH, D_MODEL), f32) * POOL_WIDTH ** -0.5,
    }


def reference(x_prompt, x_sample, cache_k, cache_v, cache_kidx, state_pool, norm_w, final_norm_w,
              attn_w_in, attn_w_out, rel_bias, pool_w_in, pool_group_w, pool_scale, pool_w_out):
    xp, xs = x_prompt, x_sample
    kp, vp, kip, poolp = [], [], [], []
    ksm, vsm, kism, poolsm = [], [], [], []
    for i in range(DEPTH):
        hp = rmsnorm(xp, norm_w[i])
        hs = rmsnorm(xs, norm_w[i])
        if i % 2 == 0:
            a = i // 2
            q, k, v, qi, ki, wi, g = attn_project(hp, attn_w_in[a])
            o = prompt_attend(q, qi, wi, k, v, ki, rel_bias)
            xp = xp + (o * jax.nn.silu(g)) @ attn_w_out[a]
            kp.append(k); vp.append(v); kip.append(ki)
            q, k, v, qi, ki, wi, g = attn_project(hs, attn_w_in[a])
            o = sample_attend(q, qi, wi, k, v, ki, cache_k[a], cache_v[a], cache_kidx[a], rel_bias)
            xs = xs + (o * jax.nn.silu(g)) @ attn_w_out[a]
            ksm.append(k); vsm.append(v); kism.append(ki)
        else:
            p = i // 2
            zero_hist = jnp.zeros((xp.shape[0], POOL_HIST, POOL_WIDTH), xp.dtype)
            out, hist = pool_layer(hp, zero_hist, 0, pool_w_in[p], pool_group_w[p], pool_scale[p], pool_w_out[p])
            xp = xp + out
            poolp.append(hist)
            out, hist = pool_layer(hs, state_pool[p], PAST_LEN, pool_w_in[p], pool_group_w[p], pool_scale[p], pool_w_out[p])
            xs = xs + out
            poolsm.append(hist)
    y_prompt = rmsnorm(xp, final_norm_w)
    y_sample = rmsnorm(xs, final_norm_w)
    return (y_prompt, y_sample, jnp.stack(kp), jnp.stack(vp), jnp.stack(kip), jnp.stack(poolp),
            jnp.stack(ksm), jnp.stack(vsm), jnp.stack(kism), jnp.stack(poolsm))
```

```python
import functools
import math

import jax
import jax.numpy as jnp
from jax import lax
from jax.experimental import pallas as pl
from jax.experimental.pallas import tpu as pltpu

CHUNK = 64
N_HEADS = 32
HEAD_DIM = 128
N_KV = 8
GROUP = N_HEADS // N_KV
H_IDX = 32
D_IDX = 64
TOPK_MAX = 256
NUM_BUCKETS = 32
MAX_DISTANCE = 128
POOL_WINDOWS = (2, 4, 8, 16)
N_POOL_GROUPS = len(POOL_WINDOWS)
POOL_HIST = max(POOL_WINDOWS) - 1
EPS = 1e-6

ATTN_TQ = 256
ATTN_TK = 256
HALO = 16
VMEM_LIMIT = 52 * 1024 * 1024

f32 = jnp.float32
bf16 = jnp.bfloat16
INT_MIN = -(2 ** 31)
NEG = -1e30


def _cparams(sem):
    return pltpu.CompilerParams(dimension_semantics=sem, vmem_limit_bytes=VMEM_LIMIT)


def _rmsnorm_kernel(x_ref, w_ref, o_ref):
    x = x_ref[...]
    y = x * lax.rsqrt(jnp.mean(x * x, axis=-1, keepdims=True) + EPS)
    o_ref[...] = (y * w_ref[...]).astype(o_ref.dtype)


def _rmsnorm(x, w, out_dtype, tm=256):
    m, d = x.shape
    tm = min(tm, m)
    return pl.pallas_call(
        _rmsnorm_kernel,
        grid=(m // tm,),
        in_specs=[pl.BlockSpec((tm, d), lambda i: (i, 0)), pl.BlockSpec((1, d), lambda i: (0, 0))],
        out_specs=pl.BlockSpec((tm, d), lambda i: (i, 0)),
        out_shape=jax.ShapeDtypeStruct((m, d), out_dtype),
        compiler_params=_cparams(("parallel",)),
        name="rmsnorm",
    )(x, w.reshape(1, d))


def _mm_kernel(a_ref, w_ref, o_ref):
    o_ref[...] = jnp.dot(a_ref[...], w_ref[...], preferred_element_type=f32).astype(o_ref.dtype)


def _mm_res_kernel(a_ref, w_ref, r_ref, o_ref):
    acc = jnp.dot(a_ref[...], w_ref[...], preferred_element_type=f32)
    o_ref[...] = (r_ref[...] + acc).astype(o_ref.dtype)


def _matmul(a, w, out_dtype, res=None, tm=1024, tn=512):
    m, k = a.shape
    n = w.shape[1]
    tm = min(tm, m)
    tn = min(tn, n)
    in_specs = [pl.BlockSpec((tm, k), lambda i, j: (i, 0)), pl.BlockSpec((k, tn), lambda i, j: (0, j))]
    args = [a, w]
    kern = _mm_kernel
    if res is not None:
        in_specs.append(pl.BlockSpec((tm, tn), lambda i, j: (i, j)))
        args.append(res)
        kern = _mm_res_kernel
    return pl.pallas_call(
        kern,
        grid=(m // tm, n // tn),
        in_specs=in_specs,
        out_specs=pl.BlockSpec((tm, tn), lambda i, j: (i, j)),
        out_shape=jax.ShapeDtypeStruct((m, n), out_dtype),
        compiler_params=_cparams(("parallel", "parallel")),
        name="matmul_res" if res is not None else "matmul",
    )(*args)


def _rel_bucket(rel):
    nb = NUM_BUCKETS // 2
    ret = jnp.where(rel > 0, nb, 0)
    n = jnp.abs(rel)
    max_exact = nb // 2
    nf = jnp.maximum(n, 1).astype(f32)
    large = max_exact + (jnp.log(nf / max_exact) / math.log(MAX_DISTANCE / max_exact)
                         * (nb - max_exact)).astype(jnp.int32)
    large = jnp.minimum(large, nb - 1)
    return ret + jnp.where(n < max_exact, n, large)


def _bias_table_kernel(rb_ref, bk_ref, o_ref):
    h = pl.program_id(1)
    bk = bk_ref[0]
    acc = jnp.zeros(bk.shape, f32)
    for b in range(NUM_BUCKETS):
        acc = jnp.where(bk == b, rb_ref[b, h], acc)
    o_ref[0, 0] = acc


def _bias_tables(rel_bias):
    qo = lax.broadcasted_iota(jnp.int32, (ATTN_TQ, ATTN_TK), 0)
    ko = lax.broadcasted_iota(jnp.int32, (ATTN_TQ, ATTN_TK), 1)
    buckets = jnp.stack([_rel_bucket(ko - qo), _rel_bucket(ko - ATTN_TK - qo)])
    return pl.pallas_call(
        _bias_table_kernel,
        grid=(2, N_HEADS),
        in_specs=[pl.BlockSpec(memory_space=pltpu.SMEM),
                  pl.BlockSpec((1, ATTN_TQ, ATTN_TK), lambda a, h: (a, 0, 0))],
        out_specs=pl.BlockSpec((1, 1, ATTN_TQ, ATTN_TK), lambda a, h: (a, h, 0, 0)),
        out_shape=jax.ShapeDtypeStruct((2, N_HEADS, ATTN_TQ, ATTN_TK), f32),
        compiler_params=_cparams(("parallel", "parallel")),
        name="bias_tables",
    )(rel_bias, buckets)


def _score_keys(qi_ref, w, kt):
    tq = qi_ref.shape[1]
    heads_per_dot = 8
    acc = jnp.zeros((tq, kt.shape[1]), f32)
    for hc in range(H_IDX // heads_per_dot):
        lhs = qi_ref[hc * heads_per_dot:(hc + 1) * heads_per_dot].reshape(heads_per_dot * tq, D_IDX)
        d = jnp.dot(lhs, kt, preferred_element_type=f32)
        for hh in range(heads_per_dot):
            h = hc * heads_per_dot + hh
            acc = acc + jnp.maximum(d[hh * tq:(hh + 1) * tq], 0.0) * w[:, h:h + 1]
    acc = acc + 0.0
    bits = pltpu.bitcast(acc, jnp.int32)
    return bits ^ ((bits >> 31) & jnp.int32(0x7FFFFFFF))


def _kth_largest_key(key_scr, n_tiles, k_sel):
    tq, tk = key_scr.shape[1], key_scr.shape[2]

    def bit_step(bi, tvec):
        cand = tvec + lax.shift_left(jnp.int32(1), 31 - bi)

        def count_tile(t, c):
            one = jnp.where(key_scr[t] >= cand, 1.0, 0.0)
            for s in range(tk // 128):
                c = c + one[:, s * 128:(s + 1) * 128]
            return c

        c = lax.fori_loop(0, n_tiles, count_tile, jnp.zeros((tq, 128), f32))
        cnt = jnp.sum(c, axis=1, keepdims=True)
        return jnp.where(cnt >= float(k_sel), cand, tvec)

    return lax.fori_loop(0, 32, bit_step, jnp.full((tq, 1), INT_MIN, jnp.int32))


def _write_mask_bias(key_scr, mb_scr, n_tiles, tvec):
    def body(t, carry):
        mb_scr[t] = jnp.where(key_scr[t] >= tvec, 0.0, NEG)
        return carry

    lax.fori_loop(0, n_tiles, body, 0)


def _stack_heads(x, width):
    return jnp.concatenate([x[:, hh * width:(hh + 1) * width] for hh in range(GROUP)], axis=0)


def _attend_step(qs, kt, vt, mbt, biases, m_scr, l_scr, acc_scr):
    tq = mbt.shape[0]
    s = lax.dot_general(qs, kt, (((1,), (1,)), ((), ())), preferred_element_type=f32)
    scale = HEAD_DIM ** -0.5
    lg = jnp.concatenate(
        [s[hh * tq:(hh + 1) * tq] * scale + biases[hh] + mbt for hh in range(GROUP)], axis=0)
    m_prev = m_scr[...]
    m_new = jnp.maximum(m_prev, jnp.max(lg, axis=1, keepdims=True))
    alpha = jnp.exp(m_prev - m_new)
    p = jnp.exp(lg - m_new)
    l_scr[...] = alpha * l_scr[...] + jnp.sum(p, axis=1, keepdims=True)
    acc_scr[...] = alpha * acc_scr[...] + jnp.dot(p.astype(bf16), vt, preferred_element_type=f32)
    m_scr[...] = m_new


def _init_softmax(m_scr, l_scr, acc_scr):
    m_scr[...] = jnp.full(m_scr.shape, NEG, f32)
    l_scr[...] = jnp.zeros(l_scr.shape, f32)
    acc_scr[...] = jnp.zeros(acc_scr.shape, f32)


def _finish_attention(gate, o_ref_view, l_scr, acc_scr):
    tq = gate.shape[0]
    o = acc_scr[...] / l_scr[...]
    for hh in range(GROUP):
        gh = gate[:, hh * HEAD_DIM:(hh + 1) * HEAD_DIM].astype(f32)
        oh = o[hh * tq:(hh + 1) * tq].astype(f32)
        o_ref_view[:, hh * HEAD_DIM:(hh + 1) * HEAD_DIM] = (oh * (gh * jax.nn.sigmoid(gh))).astype(o_ref_view.dtype)


def _attn_prompt_kernel(rb_ref, qi_ref, wi_ref, q_ref, gate_ref, kit_ref, k_ref, v_ref, tab_ref, o_ref,
                        key_scr, mb_scr, m_scr, l_scr, acc_scr, *, k_sel):
    i = pl.program_id(1)
    g = pl.program_id(2)
    tq, tk = ATTN_TQ, ATTN_TK
    qo = lax.broadcasted_iota(jnp.int32, (tq, tk), 0)
    ko = lax.broadcasted_iota(jnp.int32, (tq, tk), 1)
    causal = (ko // CHUNK) <= (qo // CHUNK)

    @pl.when(g == 0)
    def _select():
        w = wi_ref[0] * (D_IDX ** -0.5 * H_IDX ** -0.5)

        def score_tile(t, carry):
            key_scr[t] = _score_keys(qi_ref.at[0], w, kit_ref[0, t])
            return carry

        lax.fori_loop(0, i + 1, score_tile, 0)
        key_scr[i] = jnp.where(causal, key_scr[i], INT_MIN)
        tvec = _kth_largest_key(key_scr, i + 1, k_sel)
        _write_mask_bias(key_scr, mb_scr, i + 1, tvec)
        mb_scr[i] = jnp.where(causal, mb_scr[i], NEG)

    qs = _stack_heads(q_ref[0], HEAD_DIM)
    _init_softmax(m_scr, l_scr, acc_scr)
    far_bias = [rb_ref[NUM_BUCKETS // 2 - 1, g * GROUP + hh] for hh in range(GROUP)]

    def far_step(t, carry):
        _attend_step(qs, k_ref[0, 0, t], v_ref[0, 0, t], mb_scr[t], far_bias, m_scr, l_scr, acc_scr)
        return carry

    lax.fori_loop(0, jnp.maximum(i - 1, 0), far_step, 0)

    @pl.when(i >= 1)
    def _prev_tile():
        t = i - 1
        _attend_step(qs, k_ref[0, 0, t], v_ref[0, 0, t], mb_scr[t],
                     [tab_ref[1, hh] for hh in range(GROUP)], m_scr, l_scr, acc_scr)

    _attend_step(qs, k_ref[0, 0, i], v_ref[0, 0, i], mb_scr[i],
                 [tab_ref[0, hh] for hh in range(GROUP)], m_scr, l_scr, acc_scr)
    _finish_attention(gate_ref[0], o_ref.at[0], l_scr, acc_scr)


def _attn_prompt(rel_bias, tab, qi_hm, wi, q, gate, kit, k_hm, v_hm):
    b, s, width = q.shape
    tq, tk = ATTN_TQ, ATTN_TK
    nt = s // tk
    k_sel = min(TOPK_MAX, s // 4)
    gw = GROUP * HEAD_DIM
    return pl.pallas_call(
        functools.partial(_attn_prompt_kernel, k_sel=k_sel),
        grid=(b, s // tq, N_KV),
        in_specs=[
            pl.BlockSpec(memory_space=pltpu.SMEM),
            pl.BlockSpec((1, H_IDX, tq, D_IDX), lambda bb, i, g: (bb, 0, i, 0)),
            pl.BlockSpec((1, tq, H_IDX), lambda bb, i, g: (bb, i, 0)),
            pl.BlockSpec((1, tq, gw), lambda bb, i, g: (bb, i, g)),
            pl.BlockSpec((1, tq, gw), lambda bb, i, g: (bb, i, g)),
            pl.BlockSpec((1, nt, D_IDX, tk), lambda bb, i, g: (bb, 0, 0, 0)),
            pl.BlockSpec((1, 1, nt, tk, HEAD_DIM), lambda bb, i, g: (bb, g, 0, 0, 0)),
            pl.BlockSpec((1, 1, nt, tk, HEAD_DIM), lambda bb, i, g: (bb, g, 0, 0, 0)),
            pl.BlockSpec((2, GROUP, tq, tk), lambda bb, i, g: (0, g, 0, 0)),
        ],
        out_specs=pl.BlockSpec((1, tq, gw), lambda bb, i, g: (bb, i, g)),
        out_shape=jax.ShapeDtypeStruct((b, s, width), bf16),
        scratch_shapes=[
            pltpu.VMEM((nt, tq, tk), jnp.int32),
            pltpu.VMEM((nt, tq, tk), f32),
            pltpu.VMEM((GROUP * tq, 1), f32),
            pltpu.VMEM((GROUP * tq, 1), f32),
            pltpu.VMEM((GROUP * tq, HEAD_DIM), f32),
        ],
        compiler_params=_cparams(("parallel", "arbitrary", "arbitrary")),
        name="attn_prompt",
    )(rel_bias, qi_hm, wi, q, gate, kit, k_hm, v_hm, tab)


def _attn_sample_kernel(rb_ref, qi_ref, wi_ref, q_ref, gate_ref, ckit_ref, nkit_ref, ck_ref, cv_ref,
                        nk_ref, nv_ref, tab_ref, o_ref, key_scr, mb_scr, m_scr, l_scr, acc_scr,
                        *, k_sel, n_new):
    g = pl.program_id(1)
    tq = qi_ref.shape[2]
    tk = ATTN_TK
    nct = ckit_ref.shape[1]
    ko = lax.broadcasted_iota(jnp.int32, (tq, tk), 1)
    real = ko < n_new

    @pl.when(g == 0)
    def _select():
        w = wi_ref[0] * (D_IDX ** -0.5 * H_IDX ** -0.5)

        def score_tile(t, carry):
            key_scr[t] = _score_keys(qi_ref.at[0], w, ckit_ref[0, t])
            return carry

        lax.fori_loop(0, nct, score_tile, 0)
        key_scr[nct] = jnp.where(real, _score_keys(qi_ref.at[0], w, nkit_ref[0]), INT_MIN)
        tvec = _kth_largest_key(key_scr, nct + 1, k_sel)
        _write_mask_bias(key_scr, mb_scr, nct + 1, tvec)
        mb_scr[nct] = jnp.where(real, mb_scr[nct], NEG)

    qs = _stack_heads(q_ref[0], HEAD_DIM)
    _init_softmax(m_scr, l_scr, acc_scr)
    far_bias = [rb_ref[NUM_BUCKETS // 2 - 1, g * GROUP + hh] for hh in range(GROUP)]

    def far_step(t, carry):
        off = pl.multiple_of(t * tk, tk)
        _attend_step(qs, ck_ref[0, pl.ds(off, tk), :].astype(bf16), cv_ref[0, pl.ds(off, tk), :].astype(bf16),
                     mb_scr[t], far_bias, m_scr, l_scr, acc_scr)
        return carry

    lax.fori_loop(0, nct - 1, far_step, 0)
    last = (nct - 1) * tk
    _attend_step(qs, ck_ref[0, last:last + tk, :].astype(bf16), cv_ref[0, last:last + tk, :].astype(bf16),
                 mb_scr[nct - 1], [tab_ref[1, hh] for hh in range(GROUP)], m_scr, l_scr, acc_scr)
    _attend_step(qs, nk_ref[0], nv_ref[0], mb_scr[nct],
                 [tab_ref[0, hh] for hh in range(GROUP)], m_scr, l_scr, acc_scr)
    _finish_attention(gate_ref[0], o_ref.at[0], l_scr, acc_scr)


def _attn_sample(rel_bias, tab, qi_hm, wi, q, gate, ckit, nkit, ck, cv, nk, nv, n_new):
    b, t, width = q.shape
    tk = ATTN_TK
    p = ck.shape[1]
    nct = p // tk
    k_sel = min(TOPK_MAX, (p + n_new) // 4)
    gw = GROUP * HEAD_DIM
    return pl.pallas_call(
        functools.partial(_attn_sample_kernel, k_sel=k_sel, n_new=n_new),
        grid=(b, N_KV),
        in_specs=[
            pl.BlockSpec(memory_space=pltpu.SMEM),
            pl.BlockSpec((1, H_IDX, t, D_IDX), lambda bb, g: (bb, 0, 0, 0)),
            pl.BlockSpec((1, t, H_IDX), lambda bb, g: (bb, 0, 0)),
            pl.BlockSpec((1, t, gw), lambda bb, g: (bb, 0, g)),
            pl.BlockSpec((1, t, gw), lambda bb, g: (bb, 0, g)),
            pl.BlockSpec((1, nct, D_IDX, tk), lambda bb, g: (bb, 0, 0, 0)),
            pl.BlockSpec((1, D_IDX, tk), lambda bb, g: (bb, 0, 0)),
            pl.BlockSpec((1, p, HEAD_DIM), lambda bb, g: (bb, 0, g)),
            pl.BlockSpec((1, p, HEAD_DIM), lambda bb, g: (bb, 0, g)),
            pl.BlockSpec((1, tk, HEAD_DIM), lambda bb, g: (bb, 0, g)),
            pl.BlockSpec((1, tk, HEAD_DIM), lambda bb, g: (bb, 0, g)),
            pl.BlockSpec((2, GROUP, t, tk), lambda bb, g: (0, g, 0, 0)),
        ],
        out_specs=pl.BlockSpec((1, t, gw), lambda bb, g: (bb, 0, g)),
        out_shape=jax.ShapeDtypeStruct((b, t, width), bf16),
        scratch_shapes=[
            pltpu.VMEM((nct + 1, t, tk), jnp.int32),
            pltpu.VMEM((nct + 1, t, tk), f32),
            pltpu.VMEM((GROUP * t, 1), f32),
            pltpu.VMEM((GROUP * t, 1), f32),
            pltpu.VMEM((GROUP * t, HEAD_DIM), f32),
        ],
        compiler_params=_cparams(("parallel", "arbitrary")),
        name="attn_sample",
    )(rel_bias, qi_hm, wi, q, gate, ckit, nkit, ck, cv, nk, nv, tab)


def _pool_kernel(u_ref, hist_ref, gate_ref, gw_ref, scale_ref, o_ref, ext_scr, *, start):
    i = pl.program_id(1)
    tm = u_ref.shape[1]
    gwid = gw_ref.shape[1]

    @pl.when(i == 0)
    def _first():
        ext_scr[0:HALO, :] = hist_ref[0]

    @pl.when(i > 0)
    def _carry():
        ext_scr[0:HALO, :] = ext_scr[tm:tm + HALO, :]

    ext_scr[HALO:HALO + tm, :] = u_ref[0]
    pos = start + i * tm + lax.broadcasted_iota(jnp.int32, (tm, 1), 0)
    for gi, win in enumerate(POOL_WINDOWS):
        lo, hi = gi * gwid, (gi + 1) * gwid
        tok = ext_scr[HALO:HALO + tm, lo:hi]
        s = tok
        for j in range(1, win):
            s = s + ext_scr[HALO - j:HALO - j + tm, lo:hi]
        cnt = jnp.minimum(pos + 1, win).astype(f32)
        mix = (s / cnt - tok).astype(bf16)
        z = jnp.dot(mix, gw_ref[gi], preferred_element_type=f32) * scale_ref[:, lo:hi]
        gt = gate_ref[0, :, lo:hi].astype(f32)
        o_ref[0, :, lo:hi] = (z * (gt * jax.nn.sigmoid(gt))).astype(o_ref.dtype)


def _pool_mix(u, hist16, gate, group_w, scale, start, tm=256):
    b, t, e = u.shape
    tm = min(tm, t)
    gwid = e // N_POOL_GROUPS
    return pl.pallas_call(
        functools.partial(_pool_kernel, start=start),
        grid=(b, t // tm),
        in_specs=[
            pl.BlockSpec((1, tm, e), lambda bb, i: (bb, i, 0)),
            pl.BlockSpec((1, HALO, e), lambda bb, i: (bb, 0, 0)),
            pl.BlockSpec((1, tm, e), lambda bb, i: (bb, i, 0)),
            pl.BlockSpec((N_POOL_GROUPS, gwid, gwid), lambda bb, i: (0, 0, 0)),
            pl.BlockSpec((1, e), lambda bb, i: (0, 0)),
        ],
        out_specs=pl.BlockSpec((1, tm, e), lambda bb, i: (bb, i, 0)),
        out_shape=jax.ShapeDtypeStruct((b, t, e), bf16),
        scratch_shapes=[pltpu.VMEM((HALO + tm, e), f32)],
        compiler_params=_cparams(("parallel", "arbitrary")),
        name="pool_mix",
    )(u, hist16, gate, group_w, scale)


def _attn_weights(w_in):
    d = w_in.shape[0]
    qw, kw_ = N_HEADS * HEAD_DIM, N_KV * HEAD_DIM
    o = 0
    parts = {}
    for name, wdt in (("q", qw), ("k", kw_), ("v", kw_), ("qi", H_IDX * D_IDX), ("kiwi", D_IDX + H_IDX),
                      ("gate", qw)):
        parts[name] = w_in[:, o:o + wdt].astype(bf16)
        o += wdt
    parts["kiwi"] = jnp.pad(parts["kiwi"], ((0, 0), (0, 128 - (D_IDX + H_IDX))))
    return parts


def _attn_project(h, wts, b, t):
    q = _matmul(h, wts["q"], bf16).reshape(b, t, -1)
    k = _matmul(h, wts["k"], f32).reshape(b, t, N_KV, HEAD_DIM)
    v = _matmul(h, wts["v"], f32).reshape(b, t, N_KV, HEAD_DIM)
    qi = _matmul(h, wts["qi"], bf16).reshape(b, t, H_IDX, D_IDX)
    kiwi = _matmul(h, wts["kiwi"], f32).reshape(b, t, 128)
    gate = _matmul(h, wts["gate"], bf16).reshape(b, t, -1)
    ki = kiwi[..., :D_IDX]
    wi = kiwi[..., D_IDX:D_IDX + H_IDX]
    return q, k, v, jnp.transpose(qi, (0, 2, 1, 3)), ki, wi, gate


def _key_tiles_t(ki):
    b, l, d = ki.shape
    return jnp.transpose(ki.astype(bf16).reshape(b, l // ATTN_TK, ATTN_TK, d), (0, 1, 3, 2))


def _pad_rows(x, rows):
    return jnp.pad(x, ((0, 0), (0, rows - x.shape[1])) + ((0, 0),) * (x.ndim - 2))


def kernel(x_prompt, x_sample, cache_k, cache_v, cache_kidx, state_pool, norm_w, final_norm_w,
           attn_w_in, attn_w_out, rel_bias, pool_w_in, pool_group_w, pool_scale, pool_w_out):
    bp, sp, d = x_prompt.shape
    bs, ts, _ = x_sample.shape
    past = cache_k.shape[2]
    assert sp % ATTN_TQ == 0 and past % ATTN_TK == 0 and past >= 2 * ATTN_TK
    assert past % CHUNK == 0 and ts <= CHUNK and ts % 16 == 0 and ts >= POOL_HIST
    depth = norm_w.shape[0]

    xp = x_prompt.reshape(bp * sp, d)
    xs = x_sample.reshape(bs * ts, d)
    tab = _bias_tables(rel_bias)
    outs = {name: [] for name in ("kp", "vp", "kip", "poolp", "ks", "vs", "kis", "pools")}

    for layer in range(depth):
        hp = _rmsnorm(xp, norm_w[layer], bf16)
        hs = _rmsnorm(xs, norm_w[layer], bf16)
        if layer % 2 == 0:
            a = layer // 2
            wts = _attn_weights(attn_w_in[a])
            w_out = attn_w_out[a].astype(bf16)

            q, k, v, qi_hm, ki, wi, gate = _attn_project(hp, wts, bp, sp)
            nt = sp // ATTN_TK
            k_hm = jnp.transpose(k.astype(bf16), (0, 2, 1, 3)).reshape(bp, N_KV, nt, ATTN_TK, HEAD_DIM)
            v_hm = jnp.transpose(v.astype(bf16), (0, 2, 1, 3)).reshape(bp, N_KV, nt, ATTN_TK, HEAD_DIM)
            og = _attn_prompt(rel_bias, tab, qi_hm, wi, q, gate, _key_tiles_t(ki), k_hm, v_hm)
            xp = _matmul(og.reshape(bp * sp, -1), w_out, f32, res=xp)
            outs["kp"].append(k); outs["vp"].append(v); outs["kip"].append(ki)

            q, k, v, qi_hm, ki, wi, gate = _attn_project(hs, wts, bs, ts)
            ckit = _key_tiles_t(cache_kidx[a])
            nkit = jnp.transpose(_pad_rows(ki.astype(bf16), ATTN_TK), (0, 2, 1))
            nk = _pad_rows(k.astype(bf16).reshape(bs, ts, -1), ATTN_TK)
            nv = _pad_rows(v.astype(bf16).reshape(bs, ts, -1), ATTN_TK)
            og = _attn_sample(rel_bias, tab, qi_hm, wi, q, gate, ckit, nkit,
                              cache_k[a].reshape(bs, past, -1), cache_v[a].reshape(bs, past, -1), nk, nv, ts)
            xs = _matmul(og.reshape(bs * ts, -1), w_out, f32, res=xs)
            outs["ks"].append(k); outs["vs"].append(v); outs["kis"].append(ki)
        else:
            p = layer // 2
            e = pool_w_in.shape[2] // 2
            w_u = pool_w_in[p][:, :e].astype(bf16)
            w_g = pool_w_in[p][:, e:].astype(bf16)
            gw = pool_group_w[p].astype(bf16)
            scale = pool_scale[p].reshape(1, e)
            w_out = pool_w_out[p].astype(bf16)

            u = _matmul(hp, w_u, f32).reshape(bp, sp, e)
            gate = _matmul(hp, w_g, bf16).reshape(bp, sp, e)
            zg = _pool_mix(u, jnp.zeros((bp, HALO, e), f32), gate, gw, scale, 0)
            xp = _matmul(zg.reshape(bp * sp, e), w_out, f32, res=xp)
            outs["poolp"].append(u[:, sp - POOL_HIST:])

            u = _matmul(hs, w_u, f32).reshape(bs, ts, e)
            gate = _matmul(hs, w_g, bf16).reshape(bs, ts, e)
            hist16 = jnp.pad(state_pool[p], ((0, 0), (HALO - POOL_HIST, 0), (0, 0)))
            zg = _pool_mix(u, hist16, gate, gw, scale, past)
            xs = _matmul(zg.reshape(bs * ts, e), w_out, f32, res=xs)
            outs["pools"].append(u[:, ts - POOL_HIST:])

    y_prompt = _rmsnorm(xp, final_norm_w, f32).reshape(bp, sp, d)
    y_sample = _rmsnorm(xs, final_norm_w, f32).reshape(bs, ts, d)
    return (y_prompt, y_sample, jnp.stack(outs["kp"]), jnp.stack(outs["vp"]), jnp.stack(outs["kip"]),
            jnp.stack(outs["poolp"]), jnp.stack(outs["ks"]), jnp.stack(outs["vs"]), jnp.stack(outs["kis"]),
            jnp.stack(outs["pools"]))
```

```python
import functools
import math

import jax
import jax.numpy as jnp
from jax import lax
from jax.experimental import pallas as pl
from jax.experimental.pallas import tpu as pltpu

CHUNK = 64
N_HEADS = 32
HEAD_DIM = 128
N_KV = 8
GROUP = N_HEADS // N_KV
H_IDX = 32
D_IDX = 64
TOPK_MAX = 256
NUM_BUCKETS = 32
MAX_DISTANCE = 128
FAR_BUCKET = NUM_BUCKETS // 2 - 1
POOL_WINDOWS = (2, 4, 8, 16)
N_POOL_GROUPS = len(POOL_WINDOWS)
POOL_HIST = max(POOL_WINDOWS) - 1
EPS = 1e-6

ATTN_TQ = 256
ATTN_TK = 256
SAMPLE_TK = 512
HALO = 16
VMEM_LIMIT = 52 * 1024 * 1024

f32 = jnp.float32
bf16 = jnp.bfloat16
INT_MIN = -(2 ** 31)
NEG = -1e30
SM_SCALE = HEAD_DIM ** -0.5


def _cparams(sem):
    return pltpu.CompilerParams(dimension_semantics=sem, vmem_limit_bytes=VMEM_LIMIT)


def _rmsnorm_kernel(x_ref, w_ref, o_ref):
    x = x_ref[...]
    y = x * lax.rsqrt(jnp.mean(x * x, axis=-1, keepdims=True) + EPS)
    o_ref[...] = (y * w_ref[...]).astype(o_ref.dtype)


def _rmsnorm(x, w, out_dtype, tm=256):
    m, d = x.shape
    tm = min(tm, m)
    assert m % tm == 0
    return pl.pallas_call(
        _rmsnorm_kernel,
        grid=(m // tm,),
        in_specs=[pl.BlockSpec((tm, d), lambda i: (i, 0)), pl.BlockSpec((1, d), lambda i: (0, 0))],
        out_specs=pl.BlockSpec((tm, d), lambda i: (i, 0)),
        out_shape=jax.ShapeDtypeStruct((m, d), out_dtype),
        compiler_params=_cparams(("parallel",)),
        name="rmsnorm",
    )(x, w.reshape(1, d))


def _mm_kernel(a_ref, w_ref, o_ref):
    o_ref[...] = jnp.dot(a_ref[...], w_ref[...], preferred_element_type=f32).astype(o_ref.dtype)


def _mm_res_kernel(a_ref, w_ref, r_ref, o_ref):
    acc = jnp.dot(a_ref[...], w_ref[...], preferred_element_type=f32)
    o_ref[...] = (r_ref[...] + acc).astype(o_ref.dtype)


def _matmul(a, w, out_dtype, res=None, tm=1024, tn=512):
    m, k = a.shape
    n = w.shape[1]
    tm = min(tm, m)
    tn = min(tn, n)
    assert m % tm == 0 and n % tn == 0
    in_specs = [pl.BlockSpec((tm, k), lambda i, j: (i, 0)), pl.BlockSpec((k, tn), lambda i, j: (0, j))]
    args = [a, w]
    kern = _mm_kernel
    if res is not None:
        in_specs.append(pl.BlockSpec((tm, tn), lambda i, j: (i, j)))
        args.append(res)
        kern = _mm_res_kernel
    return pl.pallas_call(
        kern,
        grid=(m // tm, n // tn),
        in_specs=in_specs,
        out_specs=pl.BlockSpec((tm, tn), lambda i, j: (i, j)),
        out_shape=jax.ShapeDtypeStruct((m, n), out_dtype),
        compiler_params=_cparams(("parallel", "parallel")),
        name="matmul_res" if res is not None else "matmul",
    )(*args)


def _rel_bucket(rel):
    nb = NUM_BUCKETS // 2
    ret = jnp.where(rel > 0, nb, 0)
    n = jnp.abs(rel)
    max_exact = nb // 2
    nf = jnp.maximum(n, 1).astype(f32)
    large = max_exact + (jnp.log(nf / max_exact) / math.log(MAX_DISTANCE / max_exact)
                         * (nb - max_exact)).astype(jnp.int32)
    large = jnp.minimum(large, nb - 1)
    return ret + jnp.where(n < max_exact, n, large)


def _bias_table_kernel(rb_ref, bk_ref, o_ref):
    h = pl.program_id(1)
    bk = bk_ref[0]
    acc = jnp.zeros(bk.shape, f32)
    for b in range(NUM_BUCKETS):
        acc = jnp.where(bk == b, rb_ref[b, h], acc)
    o_ref[0, 0] = acc - rb_ref[FAR_BUCKET, h]


def _bias_tables(rel_bias, key_major):
    ax_q, ax_k = (1, 0) if key_major else (0, 1)
    qo = lax.broadcasted_iota(jnp.int32, (ATTN_TK, ATTN_TK), ax_q)
    ko = lax.broadcasted_iota(jnp.int32, (ATTN_TK, ATTN_TK), ax_k)
    buckets = jnp.stack([_rel_bucket(ko - qo), _rel_bucket(ko - ATTN_TK - qo)])
    return pl.pallas_call(
        _bias_table_kernel,
        grid=(2, N_HEADS),
        in_specs=[pl.BlockSpec(memory_space=pltpu.SMEM),
                  pl.BlockSpec((1, ATTN_TK, ATTN_TK), lambda a, h: (a, 0, 0))],
        out_specs=pl.BlockSpec((1, 1, ATTN_TK, ATTN_TK), lambda a, h: (a, h, 0, 0)),
        out_shape=jax.ShapeDtypeStruct((2, N_HEADS, ATTN_TK, ATTN_TK), f32),
        compiler_params=_cparams(("parallel", "parallel")),
        name="bias_tables",
    )(rel_bias, buckets)


def _float_order_key(x):
    bits = pltpu.bitcast(x + 0.0, jnp.int32)
    return bits ^ ((bits >> 31) & jnp.int32(0x7FFFFFFF))


def _score_keys_km(qi_ref, w_t, ki_t):
    tq = qi_ref.shape[1]
    heads_per_dot = 8
    acc = jnp.zeros((ki_t.shape[0], tq), f32)
    for hc in range(H_IDX // heads_per_dot):
        rhs = qi_ref[hc * heads_per_dot:(hc + 1) * heads_per_dot].reshape(heads_per_dot * tq, D_IDX)
        d = lax.dot_general(ki_t, rhs, (((1,), (1,)), ((), ())), preferred_element_type=f32)
        for hh in range(heads_per_dot):
            h = hc * heads_per_dot + hh
            acc = acc + jnp.maximum(d[:, hh * tq:(hh + 1) * tq], 0.0) * w_t[h:h + 1, :]
    return _float_order_key(acc)


def _kth_largest_key_km(key_scr, n_tiles, k_sel):
    tk, tq = key_scr.shape[1], key_scr.shape[2]

    def bit_step(bi, tvec):
        cand = tvec + lax.shift_left(jnp.int32(1), 31 - bi)

        def count_tile(t, c):
            one = jnp.where(key_scr[t] >= cand, 1.0, 0.0)
            return c + jnp.sum(one.reshape(tk // 8, 8, tq), axis=0)

        c = lax.fori_loop(0, n_tiles, count_tile, jnp.zeros((8, tq), f32))
        cnt = jnp.sum(c, axis=0, keepdims=True)
        return jnp.where(cnt >= float(k_sel), cand, tvec)

    return lax.fori_loop(0, 32, bit_step, jnp.full((1, tq), INT_MIN, jnp.int32))


def _softmax_stage(s_ref, p_ref, mask_bias, pv_prev, m_scr, l_scr, acc_scr):
    lg = s_ref[...] * SM_SCALE + mask_bias
    m_prev = m_scr[...]
    m_new = jnp.maximum(m_prev, jnp.max(lg, axis=0, keepdims=True))
    alpha = jnp.exp(m_prev - m_new)
    p = jnp.exp(lg - m_new)
    l_scr[...] = alpha * l_scr[...] + jnp.sum(p, axis=0, keepdims=True)
    acc_scr[...] = (acc_scr[...] + pv_prev) * alpha
    m_scr[...] = m_new
    p_ref[...] = p.astype(bf16)


def _attn_prompt_kernel(qi_ref, wi_ref, q_ref, gate_ref, ki_ref, k_ref, vt_ref, tab_ref, o_ref,
                        key_scr, mb_scr, nb_scr, sa_scr, sb_scr, pa_scr, pb_scr, m_scr, l_scr, acc_scr,
                        *, k_sel):
    i = pl.program_id(1)
    g = pl.program_id(2)
    tq, tk = ATTN_TQ, ATTN_TK
    ko = lax.broadcasted_iota(jnp.int32, (tk, tq), 0)
    qo = lax.broadcasted_iota(jnp.int32, (tk, tq), 1)
    causal = (ko // CHUNK) <= (qo // CHUNK)

    @pl.when(g == 0)
    def _select():
        w_t = wi_ref[0] * (D_IDX ** -0.5 * H_IDX ** -0.5)

        def score_tile(t, carry):
            key_scr[t] = _score_keys_km(qi_ref.at[0], w_t, ki_ref[0, t])
            return carry

        lax.fori_loop(0, i + 1, score_tile, 0)
        key_scr[i] = jnp.where(causal, key_scr[i], INT_MIN)
        tvec = _kth_largest_key_km(key_scr, i + 1, k_sel)

        def mask_tile(t, carry):
            mb_scr[t] = jnp.where(key_scr[t] >= tvec, 0.0, NEG)
            return carry

        lax.fori_loop(0, i + 1, mask_tile, 0)
        mb_scr[i] = jnp.where(causal, mb_scr[i], NEG)

    n_far = jnp.maximum(i - 1, 0)
    n_pairs = (n_far + 1) // 2
    t_prev = jnp.maximum(i - 1, 0)
    mb_prev = jnp.where(i >= 1, mb_scr[t_prev], NEG)
    mb_diag = mb_scr[i]
    for hh in range(GROUP):
        nb_scr[0, :, hh * tq:(hh + 1) * tq] = mb_prev + tab_ref[1, hh]
        nb_scr[1, :, hh * tq:(hh + 1) * tq] = mb_diag + tab_ref[0, hh]
    m_scr[...] = jnp.full(m_scr.shape, NEG, f32)
    l_scr[...] = jnp.zeros(l_scr.shape, f32)
    acc_scr[...] = jnp.zeros(acc_scr.shape, f32)
    pb_scr[...] = jnp.zeros(pb_scr.shape, bf16)
    qs = jnp.concatenate([q_ref[0, :, hh * HEAD_DIM:(hh + 1) * HEAD_DIM] for hh in range(GROUP)], axis=0)

    def qk(t):
        return lax.dot_general(k_ref[0, 0, t], qs, (((1,), (1,)), ((), ())), preferred_element_type=f32)

    def pv(t, p_ref):
        return jnp.dot(vt_ref[0, 0, t], p_ref[...], preferred_element_type=f32)

    def far_mask(t):
        mbt = jnp.where(t < n_far, mb_scr[jnp.minimum(t, n_far - 1)], NEG)
        return jnp.concatenate([mbt] * GROUP, axis=1)

    sa_scr[...] = qk(0)

    def pair_step(tau, carry):
        t0 = 2 * tau
        pv_b = pv(jnp.maximum(t0 - 1, 0), pb_scr)
        sb_scr[...] = qk(jnp.minimum(t0 + 1, n_far - 1))
        _softmax_stage(sa_scr, pa_scr, far_mask(t0), pv_b, m_scr, l_scr, acc_scr)
        pv_a = pv(t0, pa_scr)
        sa_scr[...] = qk(jnp.minimum(t0 + 2, n_far))
        _softmax_stage(sb_scr, pb_scr, far_mask(t0 + 1), pv_a, m_scr, l_scr, acc_scr)
        return carry

    lax.fori_loop(0, n_pairs, pair_step, 0)
    pv_b = pv(jnp.clip(2 * n_pairs - 1, 0, jnp.maximum(n_far - 1, 0)), pb_scr)
    sb_scr[...] = qk(i)
    _softmax_stage(sa_scr, pa_scr, nb_scr[0], pv_b, m_scr, l_scr, acc_scr)
    pv_a = pv(t_prev, pa_scr)
    _softmax_stage(sb_scr, pb_scr, nb_scr[1], pv_a, m_scr, l_scr, acc_scr)
    o_t = (acc_scr[...] + pv(i, pb_scr)) / l_scr[...]
    for hh in range(GROUP):
        o = o_t[:, hh * tq:(hh + 1) * tq].T
        gt = gate_ref[0, :, hh * HEAD_DIM:(hh + 1) * HEAD_DIM].astype(f32)
        o_ref[0, :, hh * HEAD_DIM:(hh + 1) * HEAD_DIM] = (o * (gt * jax.nn.sigmoid(gt))).astype(o_ref.dtype)


def _attn_prompt(tab, qi_hm, wi_t, q, gate, ki, k_hm, vt_hm):
    b, s, width = q.shape
    tq, tk = ATTN_TQ, ATTN_TK
    nt = s // tk
    k_sel = min(TOPK_MAX, s // 4)
    gw = GROUP * HEAD_DIM
    return pl.pallas_call(
        functools.partial(_attn_prompt_kernel, k_sel=k_sel),
        grid=(b, s // tq, N_KV),
        in_specs=[
            pl.BlockSpec((1, H_IDX, tq, D_IDX), lambda bb, i, g: (bb, 0, i, 0)),
            pl.BlockSpec((1, H_IDX, tq), lambda bb, i, g: (bb, 0, i)),
            pl.BlockSpec((1, tq, gw), lambda bb, i, g: (bb, i, g)),
            pl.BlockSpec((1, tq, gw), lambda bb, i, g: (bb, i, g)),
            pl.BlockSpec((1, nt, tk, D_IDX), lambda bb, i, g: (bb, 0, 0, 0)),
            pl.BlockSpec((1, 1, nt, tk, HEAD_DIM), lambda bb, i, g: (bb, g, 0, 0, 0)),
            pl.BlockSpec((1, 1, nt, HEAD_DIM, tk), lambda bb, i, g: (bb, g, 0, 0, 0)),
            pl.BlockSpec((2, GROUP, tk, tq), lambda bb, i, g: (0, g, 0, 0)),
        ],
        out_specs=pl.BlockSpec((1, tq, gw), lambda bb, i, g: (bb, i, g)),
        out_shape=jax.ShapeDtypeStruct((b, s, width), bf16),
        scratch_shapes=[
            pltpu.VMEM((nt, tk, tq), jnp.int32),
            pltpu.VMEM((nt, tk, tq), f32),
            pltpu.VMEM((2, tk, GROUP * tq), f32),
            pltpu.VMEM((tk, GROUP * tq), f32),
            pltpu.VMEM((tk, GROUP * tq), f32),
            pltpu.VMEM((tk, GROUP * tq), bf16),
            pltpu.VMEM((tk, GROUP * tq), bf16),
            pltpu.VMEM((1, GROUP * tq), f32),
            pltpu.VMEM((1, GROUP * tq), f32),
            pltpu.VMEM((HEAD_DIM, GROUP * tq), f32),
        ],
        compiler_params=_cparams(("parallel", "arbitrary", "arbitrary")),
        name="attn_prompt",
    )(qi_hm, wi_t, q, gate, ki, k_hm, vt_hm, tab)


def _score_keys_qm(qi_ref, w, kt):
    tq = qi_ref.shape[1]
    heads_per_dot = 8
    acc = jnp.zeros((tq, kt.shape[1]), f32)
    for hc in range(H_IDX // heads_per_dot):
        lhs = qi_ref[hc * heads_per_dot:(hc + 1) * heads_per_dot].reshape(heads_per_dot * tq, D_IDX)
        d = jnp.dot(lhs, kt, preferred_element_type=f32)
        for hh in range(heads_per_dot):
            h = hc * heads_per_dot + hh
            acc = acc + jnp.maximum(d[hh * tq:(hh + 1) * tq], 0.0) * w[:, h:h + 1]
    return _float_order_key(acc)


def _kth_largest_key_qm(key_scr, n_tiles, k_sel):
    tq, tk = key_scr.shape[1], key_scr.shape[2]

    def bit_step(bi, tvec):
        cand = tvec + lax.shift_left(jnp.int32(1), 31 - bi)

        def count_tile(t, c):
            one = jnp.where(key_scr[t] >= cand, 1.0, 0.0)
            for s in range(tk // 128):
                c = c + one[:, s * 128:(s + 1) * 128]
            return c

        c = lax.fori_loop(0, n_tiles, count_tile, jnp.zeros((tq, 128), f32))
        cnt = jnp.sum(c, axis=1, keepdims=True)
        return jnp.where(cnt >= float(k_sel), cand, tvec)

    return lax.fori_loop(0, 32, bit_step, jnp.full((tq, 1), INT_MIN, jnp.int32))


def _select_sample_kernel(qi_ref, wi_ref, ckit_ref, nkit_ref, mb_ref, key_scr, *, k_sel, n_new):
    tq = qi_ref.shape[2]
    tk = ATTN_TK
    nct = ckit_ref.shape[1]
    real = lax.broadcasted_iota(jnp.int32, (tq, tk), 1) < n_new
    w = wi_ref[0] * (D_IDX ** -0.5 * H_IDX ** -0.5)

    def score_tile(t, carry):
        key_scr[t] = _score_keys_qm(qi_ref.at[0], w, ckit_ref[0, t])
        return carry

    lax.fori_loop(0, nct, score_tile, 0)
    key_scr[nct] = jnp.where(real, _score_keys_qm(qi_ref.at[0], w, nkit_ref[0]), INT_MIN)
    tvec = _kth_largest_key_qm(key_scr, nct + 1, k_sel)

    def mask_tile(t, carry):
        mb_ref[0, t] = jnp.where(key_scr[t] >= tvec, 0.0, NEG)
        return carry

    lax.fori_loop(0, nct, mask_tile, 0)
    mb_ref[0, nct] = jnp.where(real & (key_scr[nct] >= tvec), 0.0, NEG)


def _select_sample(qi_hm, wi, ckit, nkit, n_new):
    b, _, t, _ = qi_hm.shape
    nct = ckit.shape[1]
    tk = ATTN_TK
    k_sel = min(TOPK_MAX, (nct * tk + n_new) // 4)
    return pl.pallas_call(
        functools.partial(_select_sample_kernel, k_sel=k_sel, n_new=n_new),
        grid=(b,),
        in_specs=[
            pl.BlockSpec((1, H_IDX, t, D_IDX), lambda bb: (bb, 0, 0, 0)),
            pl.BlockSpec((1, t, H_IDX), lambda bb: (bb, 0, 0)),
            pl.BlockSpec((1, nct, D_IDX, tk), lambda bb: (bb, 0, 0, 0)),
            pl.BlockSpec((1, D_IDX, tk), lambda bb: (bb, 0, 0)),
        ],
        out_specs=pl.BlockSpec((1, nct + 1, t, tk), lambda bb: (bb, 0, 0, 0)),
        out_shape=jax.ShapeDtypeStruct((b, nct + 1, t, tk), f32),
        scratch_shapes=[pltpu.VMEM((nct + 1, t, tk), jnp.int32)],
        compiler_params=_cparams(("parallel",)),
        name="select_sample",
    )(qi_hm, wi, ckit, nkit)


def _attend_tile_qm(q_ref, k_of, v_of, mbt, bias_of, s_scr, p_scr, m_scr, l_scr, acc_scr):
    w = mbt.shape[1]
    mb4 = jnp.concatenate([mbt] * GROUP, axis=0)
    for g in range(N_KV):
        qs = jnp.concatenate(
            [q_ref[0, :, (g * GROUP + hh) * HEAD_DIM:(g * GROUP + hh + 1) * HEAD_DIM] for hh in range(GROUP)], axis=0)
        s_scr[g, :, 0:w] = lax.dot_general(qs, k_of(g), (((1,), (1,)), ((), ())), preferred_element_type=f32)
    for g in range(N_KV):
        lg = s_scr[g, :, 0:w] * SM_SCALE + mb4
        if bias_of is not None:
            lg = lg + bias_of(g)
        m_prev = m_scr[g]
        m_new = jnp.maximum(m_prev, jnp.max(lg, axis=1, keepdims=True))
        alpha = jnp.exp(m_prev - m_new)
        p = jnp.exp(lg - m_new)
        l_scr[g] = alpha * l_scr[g] + jnp.sum(p, axis=1, keepdims=True)
        acc_scr[g] = alpha * acc_scr[g]
        m_scr[g] = m_new
        p_scr[g, :, 0:w] = p.astype(bf16)
    for g in range(N_KV):
        acc_scr[g] = acc_scr[g] + jnp.dot(p_scr[g, :, 0:w], v_of(g), preferred_element_type=f32)


def _attn_sample_kernel(q_ref, gate_ref, ck_ref, cv_ref, nk_ref, nv_ref, mbc_ref, mbn_ref, tab_ref, o_ref,
                        s_scr, p_scr, m_scr, l_scr, acc_scr):
    j = pl.program_id(1)
    nj = pl.num_programs(1)
    t = q_ref.shape[1]
    tk = ATTN_TK
    halves = ck_ref.shape[1] // tk

    @pl.when(j == 0)
    def _init():
        m_scr[...] = jnp.full(m_scr.shape, NEG, f32)
        l_scr[...] = jnp.zeros(l_scr.shape, f32)
        acc_scr[...] = jnp.zeros(acc_scr.shape, f32)

    def head_cols(g):
        return slice(g * HEAD_DIM, (g + 1) * HEAD_DIM)

    def cache_tile(last):
        mbt = jnp.concatenate([mbc_ref[0, hf] for hf in range(halves)], axis=1)
        bias_of = None
        if last:
            far = jnp.zeros((t, (halves - 1) * tk), f32)

            def bias_of(g):
                return jnp.concatenate(
                    [jnp.concatenate([far, tab_ref[1, g * GROUP + hh]], axis=1) for hh in range(GROUP)], axis=0)

        _attend_tile_qm(q_ref, lambda g: ck_ref[0, :, head_cols(g)].astype(bf16),
                        lambda g: cv_ref[0, :, head_cols(g)].astype(bf16), mbt, bias_of,
                        s_scr, p_scr, m_scr, l_scr, acc_scr)

    @pl.when(j < nj - 1)
    def _far():
        cache_tile(False)

    @pl.when(j == nj - 1)
    def _near():
        cache_tile(True)
        _attend_tile_qm(q_ref, lambda g: nk_ref[0, :, head_cols(g)], lambda g: nv_ref[0, :, head_cols(g)],
                        mbn_ref[0, 0],
                        lambda g: jnp.concatenate([tab_ref[0, g * GROUP + hh] for hh in range(GROUP)], axis=0),
                        s_scr, p_scr, m_scr, l_scr, acc_scr)
        for g in range(N_KV):
            o = acc_scr[g] / l_scr[g]
            for hh in range(GROUP):
                hcols = head_cols(g * GROUP + hh)
                gt = gate_ref[0, :, hcols].astype(f32)
                o_ref[0, :, hcols] = (o[hh * t:(hh + 1) * t] * (gt * jax.nn.sigmoid(gt))).astype(o_ref.dtype)


def _attn_sample(tab, mb, q, gate, ck, cv, nk, nv):
    b, t, width = q.shape
    tk = ATTN_TK
    p = ck.shape[1]
    halves = SAMPLE_TK // tk
    nj = p // SAMPLE_TK
    nct = p // tk
    kvw = N_KV * HEAD_DIM
    return pl.pallas_call(
        _attn_sample_kernel,
        grid=(b, nj),
        in_specs=[
            pl.BlockSpec((1, t, width), lambda bb, j: (bb, 0, 0)),
            pl.BlockSpec((1, t, width), lambda bb, j: (bb, 0, 0)),
            pl.BlockSpec((1, SAMPLE_TK, kvw), lambda bb, j: (bb, j, 0)),
            pl.BlockSpec((1, SAMPLE_TK, kvw), lambda bb, j: (bb, j, 0)),
            pl.BlockSpec((1, tk, kvw), lambda bb, j: (bb, 0, 0)),
            pl.BlockSpec((1, tk, kvw), lambda bb, j: (bb, 0, 0)),
            pl.BlockSpec((1, halves, t, tk), lambda bb, j: (bb, j, 0, 0)),
            pl.BlockSpec((1, 1, t, tk), lambda bb, j: (bb, nct, 0, 0)),
            pl.BlockSpec((2, N_HEADS, t, tk), lambda bb, j: (0, 0, 0, 0)),
        ],
        out_specs=pl.BlockSpec((1, t, width), lambda bb, j: (bb, 0, 0)),
        out_shape=jax.ShapeDtypeStruct((b, t, width), bf16),
        scratch_shapes=[
            pltpu.VMEM((N_KV, GROUP * t, SAMPLE_TK), f32),
            pltpu.VMEM((N_KV, GROUP * t, SAMPLE_TK), bf16),
            pltpu.VMEM((N_KV, GROUP * t, 1), f32),
            pltpu.VMEM((N_KV, GROUP * t, 1), f32),
            pltpu.VMEM((N_KV, GROUP * t, HEAD_DIM), f32),
        ],
        compiler_params=_cparams(("parallel", "arbitrary")),
        name="attn_sample",
    )(q, gate, ck, cv, nk, nv, mb, mb, tab)


def _pool_kernel(u_ref, hist_ref, gate_ref, gw_ref, scale_ref, o_ref, ext_scr, *, start):
    i = pl.program_id(1)
    tm = u_ref.shape[1]
    gwid = gw_ref.shape[1]

    @pl.when(i == 0)
    def _first():
        ext_scr[0:HALO, :] = hist_ref[0]

    @pl.when(i > 0)
    def _carry():
        ext_scr[0:HALO, :] = ext_scr[tm:tm + HALO, :]

    ext_scr[HALO:HALO + tm, :] = u_ref[0]
    pos = start + i * tm + lax.broadcasted_iota(jnp.int32, (tm, 1), 0)
    for gi, win in enumerate(POOL_WINDOWS):
        lo, hi = gi * gwid, (gi + 1) * gwid
        tok = ext_scr[HALO:HALO + tm, lo:hi]
        s = tok
        for j in range(1, win):
            s = s + ext_scr[HALO - j:HALO - j + tm, lo:hi]
        cnt = jnp.minimum(pos + 1, win).astype(f32)
        mix = (s / cnt - tok).astype(bf16)
        z = jnp.dot(mix, gw_ref[gi], preferred_element_type=f32) * scale_ref[:, lo:hi]
        gt = gate_ref[0, :, lo:hi].astype(f32)
        o_ref[0, :, lo:hi] = (z * (gt * jax.nn.sigmoid(gt))).astype(o_ref.dtype)


def _pool_mix(u, hist16, gate, group_w, scale, start, tm=256):
    b, t, e = u.shape
    tm = min(tm, t)
    gwid = e // N_POOL_GROUPS
    return pl.pallas_call(
        functools.partial(_pool_kernel, start=start),
        grid=(b, t // tm),
        in_specs=[
            pl.BlockSpec((1, tm, e), lambda bb, i: (bb, i, 0)),
            pl.BlockSpec((1, HALO, e), lambda bb, i: (bb, 0, 0)),
            pl.BlockSpec((1, tm, e), lambda bb, i: (bb, i, 0)),
            pl.BlockSpec((N_POOL_GROUPS, gwid, gwid), lambda bb, i: (0, 0, 0)),
            pl.BlockSpec((1, e), lambda bb, i: (0, 0)),
        ],
        out_specs=pl.BlockSpec((1, tm, e), lambda bb, i: (bb, i, 0)),
        out_shape=jax.ShapeDtypeStruct((b, t, e), bf16),
        scratch_shapes=[pltpu.VMEM((HALO + tm, e), f32)],
        compiler_params=_cparams(("parallel", "arbitrary")),
        name="pool_mix",
    )(u, hist16, gate, group_w, scale)


def _attn_weights(w_in):
    qw, kw_ = N_HEADS * HEAD_DIM, N_KV * HEAD_DIM
    o = 0
    parts = {}
    for name, wdt in (("q", qw), ("k", kw_), ("v", kw_), ("qi", H_IDX * D_IDX), ("kiwi", D_IDX + H_IDX),
                      ("gate", qw)):
        parts[name] = w_in[:, o:o + wdt].astype(bf16)
        o += wdt
    parts["kiwi"] = jnp.pad(parts["kiwi"], ((0, 0), (0, 128 - (D_IDX + H_IDX))))
    return parts


def _attn_project(h, wts, b, t):
    q = _matmul(h, wts["q"], bf16).reshape(b, t, -1)
    k = _matmul(h, wts["k"], f32).reshape(b, t, N_KV, HEAD_DIM)
    v = _matmul(h, wts["v"], f32).reshape(b, t, N_KV, HEAD_DIM)
    qi = _matmul(h, wts["qi"], bf16).reshape(b, t, H_IDX, D_IDX)
    kiwi = _matmul(h, wts["kiwi"], f32).reshape(b, t, 128)
    gate = _matmul(h, wts["gate"], bf16).reshape(b, t, -1)
    ki = kiwi[..., :D_IDX]
    wi = kiwi[..., D_IDX:D_IDX + H_IDX]
    return q, k, v, jnp.transpose(qi, (0, 2, 1, 3)), ki, wi, gate


def _key_tiles_t(ki):
    b, l, d = ki.shape
    return jnp.transpose(ki.astype(bf16).reshape(b, l // ATTN_TK, ATTN_TK, d), (0, 1, 3, 2))


def _pad_rows(x, rows):
    return jnp.pad(x, ((0, 0), (0, rows - x.shape[1])) + ((0, 0),) * (x.ndim - 2))


def kernel(x_prompt, x_sample, cache_k, cache_v, cache_kidx, state_pool, norm_w, final_norm_w,
           attn_w_in, attn_w_out, rel_bias, pool_w_in, pool_group_w, pool_scale, pool_w_out):
    bp, sp, d = x_prompt.shape
    bs, ts, _ = x_sample.shape
    past = cache_k.shape[2]
    assert ATTN_TQ == ATTN_TK and ATTN_TK >= MAX_DISTANCE and SAMPLE_TK % ATTN_TK == 0
    assert sp % ATTN_TQ == 0 and past % SAMPLE_TK == 0
    assert past % CHUNK == 0 and ts <= CHUNK and ts % 16 == 0 and ts >= POOL_HIST
    depth = norm_w.shape[0]

    xp = x_prompt.reshape(bp * sp, d)
    xs = x_sample.reshape(bs * ts, d)
    tab_km = _bias_tables(rel_bias, key_major=True)
    tab_qm = _bias_tables(rel_bias, key_major=False)
    outs = {name: [] for name in ("kp", "vp", "kip", "poolp", "ks", "vs", "kis", "pools")}

    for layer in range(depth):
        hp = _rmsnorm(xp, norm_w[layer], bf16)
        hs = _rmsnorm(xs, norm_w[layer], bf16)
        if layer % 2 == 0:
            a = layer // 2
            wts = _attn_weights(attn_w_in[a])
            w_out = attn_w_out[a].astype(bf16)

            q, k, v, qi_hm, ki, wi, gate = _attn_project(hp, wts, bp, sp)
            nt = sp // ATTN_TK
            k_hm = jnp.transpose(k.astype(bf16), (0, 2, 1, 3)).reshape(bp, N_KV, nt, ATTN_TK, HEAD_DIM)
            vt_hm = jnp.transpose(v.astype(bf16).reshape(bp, nt, ATTN_TK, N_KV, HEAD_DIM), (0, 3, 1, 4, 2))
            og = _attn_prompt(tab_km, qi_hm, jnp.transpose(wi, (0, 2, 1)), q, gate,
                              ki.astype(bf16).reshape(bp, nt, ATTN_TK, D_IDX), k_hm, vt_hm)
            xp = _matmul(og.reshape(bp * sp, -1), w_out, f32, res=xp)
            outs["kp"].append(k); outs["vp"].append(v); outs["kip"].append(ki)

            q, k, v, qi_hm, ki, wi, gate = _attn_project(hs, wts, bs, ts)
            ckit = _key_tiles_t(cache_kidx[a])
            nkit = jnp.transpose(_pad_rows(ki.astype(bf16), ATTN_TK), (0, 2, 1))
            nk = _pad_rows(k.astype(bf16).reshape(bs, ts, -1), ATTN_TK)
            nv = _pad_rows(v.astype(bf16).reshape(bs, ts, -1), ATTN_TK)
            mb = _select_sample(qi_hm, wi, ckit, nkit, ts)
            og = _attn_sample(tab_qm[:, :, :ts], mb, q, gate, cache_k[a].reshape(bs, past, -1),
                              cache_v[a].reshape(bs, past, -1), nk, nv)
            xs = _matmul(og.reshape(bs * ts, -1), w_out, f32, res=xs)
            outs["ks"].append(k); outs["vs"].append(v); outs["kis"].append(ki)
        else:
            p = layer // 2
            e = pool_w_in.shape[2] // 2
            w_u = pool_w_in[p][:, :e].astype(bf16)
            w_g = pool_w_in[p][:, e:].astype(bf16)
            gw = pool_group_w[p].astype(bf16)
            scale = pool_scale[p].reshape(1, e)
            w_out = pool_w_out[p].astype(bf16)

            u = _matmul(hp, w_u, f32).reshape(bp, sp, e)
            gate = _matmul(hp, w_g, bf16).reshape(bp, sp, e)
            zg = _pool_mix(u, jnp.zeros((bp, HALO, e), f32), gate, gw, scale, 0)
            xp = _matmul(zg.reshape(bp * sp, e), w_out, f32, res=xp)
            outs["poolp"].append(u[:, sp - POOL_HIST:])

            u = _matmul(hs, w_u, f32).reshape(bs, ts, e)
            gate = _matmul(hs, w_g, bf16).reshape(bs, ts, e)
            hist16 = jnp.pad(state_pool[p], ((0, 0), (HALO - POOL_HIST, 0), (0, 0)))
            zg = _pool_mix(u, hist16, gate, gw, scale, past)
            xs = _matmul(zg.reshape(bs * ts, e), w_out, f32, res=xs)
            outs["pools"].append(u[:, ts - POOL_HIST:])

    y_prompt = _rmsnorm(xp, final_norm_w, f32).reshape(bp, sp, d)
    y_sample = _rmsnorm(xs, final_norm_w, f32).reshape(bs, ts, d)
    return (y_prompt, y_sample, jnp.stack(outs["kp"]), jnp.stack(outs["vp"]), jnp.stack(outs["kip"]),
            jnp.stack(outs["poolp"]), jnp.stack(outs["ks"]), jnp.stack(outs["vs"]), jnp.stack(outs["kis"]),
            jnp.stack(outs["pools"]))
```

```python
import functools
import math

import jax
import jax.numpy as jnp
from jax import lax
from jax.experimental import pallas as pl
from jax.experimental.pallas import tpu as pltpu

CHUNK = 64
N_HEADS = 32
HEAD_DIM = 128
N_KV = 8
GROUP = N_HEADS // N_KV
H_IDX = 32
D_IDX = 64
TOPK_MAX = 256
NUM_BUCKETS = 32
MAX_DISTANCE = 128
FAR_BUCKET = NUM_BUCKETS // 2 - 1
POOL_WINDOWS = (2, 4, 8, 16)
N_POOL_GROUPS = len(POOL_WINDOWS)
POOL_HIST = max(POOL_WINDOWS) - 1
EPS = 1e-6

ATTN_TQ = 256
ATTN_TK = 256
SAMPLE_TK = 512
SELECT_TK = 1024
HALO = 16
VMEM_LIMIT = 52 * 1024 * 1024

f32 = jnp.float32
bf16 = jnp.bfloat16
INT_MIN = -(2 ** 31)
NEG = -1e30
LOG2E = math.log2(math.e)
SM_SCALE_LOG2 = HEAD_DIM ** -0.5 * LOG2E


def _cparams(sem):
    return pltpu.CompilerParams(dimension_semantics=sem, vmem_limit_bytes=VMEM_LIMIT)


def _rmsnorm_kernel(x_ref, w_ref, o_ref):
    x = x_ref[...]
    y = x * lax.rsqrt(jnp.mean(x * x, axis=-1, keepdims=True) + EPS)
    o_ref[...] = (y * w_ref[...]).astype(o_ref.dtype)


def _rmsnorm(x, w, out_dtype, tm=256):
    m, d = x.shape
    tm = min(tm, m)
    assert m % tm == 0
    return pl.pallas_call(
        _rmsnorm_kernel,
        grid=(m // tm,),
        in_specs=[pl.BlockSpec((tm, d), lambda i: (i, 0)), pl.BlockSpec((1, d), lambda i: (0, 0))],
        out_specs=pl.BlockSpec((tm, d), lambda i: (i, 0)),
        out_shape=jax.ShapeDtypeStruct((m, d), out_dtype),
        compiler_params=_cparams(("parallel",)),
        name="rmsnorm",
    )(x, w.reshape(1, d))


def _mm_kernel(a_ref, w_ref, o_ref):
    o_ref[...] = jnp.dot(a_ref[...], w_ref[...], preferred_element_type=f32).astype(o_ref.dtype)


def _mm_res_kernel(a_ref, w_ref, r_ref, o_ref):
    acc = jnp.dot(a_ref[...], w_ref[...], preferred_element_type=f32)
    o_ref[...] = (r_ref[...] + acc).astype(o_ref.dtype)


def _matmul(a, w, out_dtype, res=None, tm=1024, tn=512):
    m, k = a.shape
    n = w.shape[1]
    tm = min(tm, m)
    tn = min(tn, n)
    assert m % tm == 0 and n % tn == 0
    in_specs = [pl.BlockSpec((tm, k), lambda i, j: (i, 0)), pl.BlockSpec((k, tn), lambda i, j: (0, j))]
    args = [a, w]
    kern = _mm_kernel
    if res is not None:
        in_specs.append(pl.BlockSpec((tm, tn), lambda i, j: (i, j)))
        args.append(res)
        kern = _mm_res_kernel
    return pl.pallas_call(
        kern,
        grid=(m // tm, n // tn),
        in_specs=in_specs,
        out_specs=pl.BlockSpec((tm, tn), lambda i, j: (i, j)),
        out_shape=jax.ShapeDtypeStruct((m, n), out_dtype),
        compiler_params=_cparams(("parallel", "parallel")),
        name="matmul_res" if res is not None else "matmul",
    )(*args)


def _mm_heads_kernel(a_ref, w_ref, o_ref):
    acc = jnp.dot(a_ref[...], w_ref[...], preferred_element_type=f32)
    dh = o_ref.shape[2]
    for hh in range(o_ref.shape[0]):
        o_ref[hh] = acc[:, hh * dh:(hh + 1) * dh].astype(o_ref.dtype)


def _matmul_heads(a, w, dh, out_dtype, tm=1024, tn=512):
    m, k = a.shape
    n = w.shape[1]
    tm = min(tm, m)
    assert m % tm == 0 and n % tn == 0 and tn % dh == 0
    return pl.pallas_call(
        _mm_heads_kernel,
        grid=(m // tm, n // tn),
        in_specs=[pl.BlockSpec((tm, k), lambda i, j: (i, 0)), pl.BlockSpec((k, tn), lambda i, j: (0, j))],
        out_specs=pl.BlockSpec((tn // dh, tm, dh), lambda i, j: (j, i, 0)),
        out_shape=jax.ShapeDtypeStruct((n // dh, m, dh), out_dtype),
        compiler_params=_cparams(("parallel", "parallel")),
        name="matmul_heads",
    )(a, w)


def _rel_bucket(rel):
    nb = NUM_BUCKETS // 2
    ret = jnp.where(rel > 0, nb, 0)
    n = jnp.abs(rel)
    max_exact = nb // 2
    nf = jnp.maximum(n, 1).astype(f32)
    large = max_exact + (jnp.log(nf / max_exact) / math.log(MAX_DISTANCE / max_exact)
                         * (nb - max_exact)).astype(jnp.int32)
    large = jnp.minimum(large, nb - 1)
    return ret + jnp.where(n < max_exact, n, large)


def _bias_table_kernel(rb_ref, bk_ref, o_ref):
    h = pl.program_id(1)
    bk = bk_ref[0]
    acc = jnp.zeros(bk.shape, f32)
    for b in range(NUM_BUCKETS):
        acc = jnp.where(bk == b, rb_ref[b, h], acc)
    o_ref[0, 0] = (acc - rb_ref[FAR_BUCKET, h]) * LOG2E


def _bias_tables(rel_bias, key_major):
    ax_q, ax_k = (1, 0) if key_major else (0, 1)
    qo = lax.broadcasted_iota(jnp.int32, (ATTN_TK, ATTN_TK), ax_q)
    ko = lax.broadcasted_iota(jnp.int32, (ATTN_TK, ATTN_TK), ax_k)
    buckets = jnp.stack([_rel_bucket(ko - qo), _rel_bucket(ko - ATTN_TK - qo)])
    return pl.pallas_call(
        _bias_table_kernel,
        grid=(2, N_HEADS),
        in_specs=[pl.BlockSpec(memory_space=pltpu.SMEM),
                  pl.BlockSpec((1, ATTN_TK, ATTN_TK), lambda a, h: (a, 0, 0))],
        out_specs=pl.BlockSpec((1, 1, ATTN_TK, ATTN_TK), lambda a, h: (a, h, 0, 0)),
        out_shape=jax.ShapeDtypeStruct((2, N_HEADS, ATTN_TK, ATTN_TK), f32),
        compiler_params=_cparams(("parallel", "parallel")),
        name="bias_tables",
    )(rel_bias, buckets)


def _float_order_key(x):
    bits = pltpu.bitcast(x + 0.0, jnp.int32)
    return bits ^ ((bits >> 31) & jnp.int32(0x7FFFFFFF))


def _score_keys_km(qi_ref, w_t, ki_t):
    tq = qi_ref.shape[1]
    heads_per_dot = 8
    acc = jnp.zeros((ki_t.shape[0], tq), f32)
    for hc in range(H_IDX // heads_per_dot):
        rhs = qi_ref[hc * heads_per_dot:(hc + 1) * heads_per_dot].reshape(heads_per_dot * tq, D_IDX)
        d = lax.dot_general(ki_t, rhs, (((1,), (1,)), ((), ())), preferred_element_type=f32)
        for hh in range(heads_per_dot):
            h = hc * heads_per_dot + hh
            acc = acc + jnp.maximum(d[:, hh * tq:(hh + 1) * tq], 0.0) * w_t[h:h + 1, :]
    return _float_order_key(acc)


def _kth_largest_key_km(key_scr, n_tiles, k_sel):
    tk, tq = key_scr.shape[1], key_scr.shape[2]

    def bit_step(bi, tvec):
        cand = tvec + lax.shift_left(jnp.int32(1), 31 - bi)

        def count_tile(t, c):
            one = jnp.where(key_scr[t] >= cand, 1.0, 0.0)
            return c + jnp.sum(one.reshape(tk // 8, 8, tq), axis=0)

        c = lax.fori_loop(0, n_tiles, count_tile, jnp.zeros((8, tq), f32))
        cnt = jnp.sum(c, axis=0, keepdims=True)
        return jnp.where(cnt >= float(k_sel), cand, tvec)

    return lax.fori_loop(0, 32, bit_step, jnp.full((1, tq), INT_MIN, jnp.int32))


def _softmax_stage(s_ref, p_ref, mask_bias, pv_prev, m_scr, l_scr, acc_scr):
    lg = s_ref[...] * SM_SCALE_LOG2 + mask_bias
    m_prev = m_scr[...]
    m_new = jnp.maximum(m_prev, jnp.max(lg, axis=0, keepdims=True))
    alpha = jnp.exp2(m_prev - m_new)
    p = jnp.exp2(lg - m_new)
    l_scr[...] = alpha * l_scr[...] + jnp.sum(p, axis=0, keepdims=True)
    acc_scr[...] = (acc_scr[...] + pv_prev) * alpha
    m_scr[...] = m_new
    p_ref[...] = p.astype(bf16)


def _attn_prompt_kernel(qi_ref, wi_ref, q_ref, gate_ref, ki_ref, k_ref, vt_ref, tab_ref, o_ref,
                        key_scr, mb_scr, nb_scr, sa_scr, sb_scr, pa_scr, pb_scr, m_scr, l_scr, acc_scr,
                        *, k_sel):
    i = pl.program_id(1)
    g = pl.program_id(2)
    tq, tk = ATTN_TQ, ATTN_TK
    ko = lax.broadcasted_iota(jnp.int32, (tk, tq), 0)
    qo = lax.broadcasted_iota(jnp.int32, (tk, tq), 1)
    causal = (ko // CHUNK) <= (qo // CHUNK)

    @pl.when(g == 0)
    def _select():
        w_t = wi_ref[0] * (D_IDX ** -0.5 * H_IDX ** -0.5)

        def score_tile(t, carry):
            key_scr[t] = _score_keys_km(qi_ref, w_t, ki_ref[0, t])
            return carry

        lax.fori_loop(0, i + 1, score_tile, 0)
        key_scr[i] = jnp.where(causal, key_scr[i], INT_MIN)
        tvec = _kth_largest_key_km(key_scr, i + 1, k_sel)

        def mask_tile(t, carry):
            mb_scr[t] = jnp.where(key_scr[t] >= tvec, 0.0, NEG)
            return carry

        lax.fori_loop(0, i + 1, mask_tile, 0)
        mb_scr[i] = jnp.where(causal, mb_scr[i], NEG)

    n_far = jnp.maximum(i - 1, 0)
    n_pairs = (n_far + 1) // 2
    t_prev = jnp.maximum(i - 1, 0)
    mb_prev = jnp.where(i >= 1, mb_scr[t_prev], NEG)
    mb_diag = mb_scr[i]
    for hh in range(GROUP):
        nb_scr[0, :, hh * tq:(hh + 1) * tq] = mb_prev + tab_ref[1, hh]
        nb_scr[1, :, hh * tq:(hh + 1) * tq] = mb_diag + tab_ref[0, hh]
    m_scr[...] = jnp.full(m_scr.shape, NEG, f32)
    l_scr[...] = jnp.zeros(l_scr.shape, f32)
    acc_scr[...] = jnp.zeros(acc_scr.shape, f32)
    pb_scr[...] = jnp.zeros(pb_scr.shape, bf16)
    qs = jnp.concatenate([q_ref[0, :, hh * HEAD_DIM:(hh + 1) * HEAD_DIM] for hh in range(GROUP)], axis=0)

    def qk(t):
        return lax.dot_general(k_ref[0, 0, t], qs, (((1,), (1,)), ((), ())), preferred_element_type=f32)

    def pv(t, p_ref):
        return jnp.dot(vt_ref[0, 0, t], p_ref[...], preferred_element_type=f32)

    def far_mask(t):
        mbt = jnp.where(t < n_far, mb_scr[jnp.minimum(t, n_far - 1)], NEG)
        return jnp.concatenate([mbt] * GROUP, axis=1)

    sa_scr[...] = qk(0)

    def pair_step(tau, carry):
        t0 = 2 * tau
        pv_b = pv(jnp.maximum(t0 - 1, 0), pb_scr)
        sb_scr[...] = qk(jnp.minimum(t0 + 1, n_far - 1))
        _softmax_stage(sa_scr, pa_scr, far_mask(t0), pv_b, m_scr, l_scr, acc_scr)
        pv_a = pv(t0, pa_scr)
        sa_scr[...] = qk(jnp.minimum(t0 + 2, n_far))
        _softmax_stage(sb_scr, pb_scr, far_mask(t0 + 1), pv_a, m_scr, l_scr, acc_scr)
        return carry

    lax.fori_loop(0, n_pairs, pair_step, 0)
    pv_b = pv(jnp.clip(2 * n_pairs - 1, 0, jnp.maximum(n_far - 1, 0)), pb_scr)
    sb_scr[...] = qk(i)
    _softmax_stage(sa_scr, pa_scr, nb_scr[0], pv_b, m_scr, l_scr, acc_scr)
    pv_a = pv(t_prev, pa_scr)
    _softmax_stage(sb_scr, pb_scr, nb_scr[1], pv_a, m_scr, l_scr, acc_scr)
    o_t = (acc_scr[...] + pv(i, pb_scr)) / l_scr[...]
    for hh in range(GROUP):
        o = o_t[:, hh * tq:(hh + 1) * tq].T
        gt = gate_ref[0, :, hh * HEAD_DIM:(hh + 1) * HEAD_DIM].astype(f32)
        o_ref[0, :, hh * HEAD_DIM:(hh + 1) * HEAD_DIM] = (o * (gt * jax.nn.sigmoid(gt))).astype(o_ref.dtype)


def _attn_prompt(tab, qi_hm, wi_t, q, gate, ki, k_hm, vt_hm):
    b, s, width = q.shape
    tq, tk = ATTN_TQ, ATTN_TK
    nt = s // tk
    nq = s // tq
    k_sel = min(TOPK_MAX, s // 4)
    gw = GROUP * HEAD_DIM
    return pl.pallas_call(
        functools.partial(_attn_prompt_kernel, k_sel=k_sel),
        grid=(b, nq, N_KV),
        in_specs=[
            pl.BlockSpec((H_IDX, tq, D_IDX), lambda bb, i, g: (0, bb * nq + i, 0)),
            pl.BlockSpec((1, H_IDX, tq), lambda bb, i, g: (bb, 0, i)),
            pl.BlockSpec((1, tq, gw), lambda bb, i, g: (bb, i, g)),
            pl.BlockSpec((1, tq, gw), lambda bb, i, g: (bb, i, g)),
            pl.BlockSpec((1, nt, tk, D_IDX), lambda bb, i, g: (bb, 0, 0, 0)),
            pl.BlockSpec((1, 1, nt, tk, HEAD_DIM), lambda bb, i, g: (bb, g, 0, 0, 0)),
            pl.BlockSpec((1, 1, nt, HEAD_DIM, tk), lambda bb, i, g: (bb, g, 0, 0, 0)),
            pl.BlockSpec((2, GROUP, tk, tq), lambda bb, i, g: (0, g, 0, 0)),
        ],
        out_specs=pl.BlockSpec((1, tq, gw), lambda bb, i, g: (bb, i, g)),
        out_shape=jax.ShapeDtypeStruct((b, s, width), bf16),
        scratch_shapes=[
            pltpu.VMEM((nt, tk, tq), jnp.int32),
            pltpu.VMEM((nt, tk, tq), f32),
            pltpu.VMEM((2, tk, GROUP * tq), f32),
            pltpu.VMEM((tk, GROUP * tq), f32),
            pltpu.VMEM((tk, GROUP * tq), f32),
            pltpu.VMEM((tk, GROUP * tq), bf16),
            pltpu.VMEM((tk, GROUP * tq), bf16),
            pltpu.VMEM((1, GROUP * tq), f32),
            pltpu.VMEM((1, GROUP * tq), f32),
            pltpu.VMEM((HEAD_DIM, GROUP * tq), f32),
        ],
        compiler_params=_cparams(("parallel", "arbitrary", "arbitrary")),
        name="attn_prompt",
    )(qi_hm, wi_t, q, gate, ki, k_hm, vt_hm, tab)


def _score_keys_qm(qi_ref, w, kt):
    tq = qi_ref.shape[1]
    heads_per_dot = 8
    acc = jnp.zeros((tq, kt.shape[1]), f32)
    for hc in range(H_IDX // heads_per_dot):
        lhs = qi_ref[hc * heads_per_dot:(hc + 1) * heads_per_dot].reshape(heads_per_dot * tq, D_IDX)
        d = jnp.dot(lhs, kt, preferred_element_type=f32)
        for hh in range(heads_per_dot):
            h = hc * heads_per_dot + hh
            acc = acc + jnp.maximum(d[hh * tq:(hh + 1) * tq], 0.0) * w[:, h:h + 1]
    return _float_order_key(acc)


def _kth_largest_key_qm(key_scr, k_sel):
    tq, width = key_scr.shape

    def bit_step(bi, tvec):
        cand = tvec + lax.shift_left(jnp.int32(1), 31 - bi)
        c = jnp.zeros((tq, 128), f32)
        for s in range(width // 128):
            c = c + jnp.where(key_scr[:, s * 128:(s + 1) * 128] >= cand, 1.0, 0.0)
        cnt = jnp.sum(c, axis=1, keepdims=True)
        return jnp.where(cnt >= float(k_sel), cand, tvec)

    return lax.fori_loop(0, 32, bit_step, jnp.full((tq, 1), INT_MIN, jnp.int32))


def _select_sample_kernel(qi_ref, wi_ref, ckit_ref, nkit_ref, mb_ref, key_scr, *, k_sel, n_new):
    tq = qi_ref.shape[1]
    tk = ATTN_TK
    p = ckit_ref.shape[2]
    real = lax.broadcasted_iota(jnp.int32, (tq, tk), 1) < n_new
    w = wi_ref[0] * (D_IDX ** -0.5 * H_IDX ** -0.5)
    for c in range(p // SELECT_TK):
        cols = slice(c * SELECT_TK, (c + 1) * SELECT_TK)
        key_scr[:, cols] = _score_keys_qm(qi_ref, w, ckit_ref[0, :, cols])
    key_scr[:, p:p + tk] = jnp.where(real, _score_keys_qm(qi_ref, w, nkit_ref[0]), INT_MIN)
    tvec = _kth_largest_key_qm(key_scr, k_sel)
    mb_ref[0, :, 0:p] = jnp.where(key_scr[:, 0:p] >= tvec, 0.0, NEG)
    mb_ref[0, :, p:p + tk] = jnp.where(real & (key_scr[:, p:p + tk] >= tvec), 0.0, NEG)


def _select_sample(qi_hm, wi, ckit, nkit, n_new):
    b, t, _ = wi.shape
    p = ckit.shape[2]
    tk = ATTN_TK
    k_sel = min(TOPK_MAX, (p + n_new) // 4)
    return pl.pallas_call(
        functools.partial(_select_sample_kernel, k_sel=k_sel, n_new=n_new),
        grid=(b,),
        in_specs=[
            pl.BlockSpec((H_IDX, t, D_IDX), lambda bb: (0, bb, 0)),
            pl.BlockSpec((1, t, H_IDX), lambda bb: (bb, 0, 0)),
            pl.BlockSpec((1, D_IDX, p), lambda bb: (bb, 0, 0)),
            pl.BlockSpec((1, D_IDX, tk), lambda bb: (bb, 0, 0)),
        ],
        out_specs=pl.BlockSpec((1, t, p + tk), lambda bb: (bb, 0, 0)),
        out_shape=jax.ShapeDtypeStruct((b, t, p + tk), f32),
        scratch_shapes=[pltpu.VMEM((t, p + tk), jnp.int32)],
        compiler_params=_cparams(("parallel",)),
        name="select_sample",
    )(qi_hm, wi, ckit, nkit)


def _attend_tile_qm(q_ref, k_of, v_of, mbt, bias_of, s_scr, p_scr, m_scr, l_scr, acc_scr):
    w = mbt.shape[1]
    mb4 = jnp.concatenate([mbt] * GROUP, axis=0)
    for g in range(N_KV):
        qs = jnp.concatenate(
            [q_ref[0, :, (g * GROUP + hh) * HEAD_DIM:(g * GROUP + hh + 1) * HEAD_DIM] for hh in range(GROUP)], axis=0)
        s_scr[g, :, 0:w] = lax.dot_general(qs, k_of(g), (((1,), (1,)), ((), ())), preferred_element_type=f32)
    for g in range(N_KV):
        lg = s_scr[g, :, 0:w] * SM_SCALE_LOG2 + mb4
        if bias_of is not None:
            lg = lg + bias_of(g)
        m_prev = m_scr[g]
        m_new = jnp.maximum(m_prev, jnp.max(lg, axis=1, keepdims=True))
        alpha = jnp.exp2(m_prev - m_new)
        p = jnp.exp2(lg - m_new)
        l_scr[g] = alpha * l_scr[g] + jnp.sum(p, axis=1, keepdims=True)
        acc_scr[g] = alpha * acc_scr[g]
        m_scr[g] = m_new
        p_scr[g, :, 0:w] = p.astype(bf16)
    for g in range(N_KV):
        acc_scr[g] = acc_scr[g] + jnp.dot(p_scr[g, :, 0:w], v_of(g), preferred_element_type=f32)


def _attn_sample_kernel(q_ref, gate_ref, ck_ref, cv_ref, nk_ref, nv_ref, mbc_ref, mbn_ref, tab_ref, o_ref,
                        s_scr, p_scr, m_scr, l_scr, acc_scr):
    j = pl.program_id(1)
    nj = pl.num_programs(1)
    t = q_ref.shape[1]
    tk = ATTN_TK
    n_keys = mbc_ref.shape[2]

    @pl.when(j == 0)
    def _init():
        m_scr[...] = jnp.full(m_scr.shape, NEG, f32)
        l_scr[...] = jnp.zeros(l_scr.shape, f32)
        acc_scr[...] = jnp.zeros(acc_scr.shape, f32)

    def head_cols(g):
        return slice(g * HEAD_DIM, (g + 1) * HEAD_DIM)

    def cache_head(ref, g):
        return ref[0, pl.ds(g, n_keys, stride=N_KV), :].astype(bf16)

    def cache_tile(last):
        bias_of = None
        if last:
            far = jnp.zeros((t, n_keys - tk), f32)

            def bias_of(g):
                return jnp.concatenate(
                    [jnp.concatenate([far, tab_ref[1, g * GROUP + hh]], axis=1) for hh in range(GROUP)], axis=0)

        _attend_tile_qm(q_ref, functools.partial(cache_head, ck_ref), functools.partial(cache_head, cv_ref),
                        mbc_ref[0], bias_of, s_scr, p_scr, m_scr, l_scr, acc_scr)

    @pl.when(j < nj - 1)
    def _far():
        cache_tile(False)

    @pl.when(j == nj - 1)
    def _near():
        cache_tile(True)
        _attend_tile_qm(q_ref, lambda g: nk_ref[0, :, head_cols(g)], lambda g: nv_ref[0, :, head_cols(g)],
                        mbn_ref[0],
                        lambda g: jnp.concatenate([tab_ref[0, g * GROUP + hh] for hh in range(GROUP)], axis=0),
                        s_scr, p_scr, m_scr, l_scr, acc_scr)
        for g in range(N_KV):
            o = acc_scr[g] / l_scr[g]
            for hh in range(GROUP):
                hcols = head_cols(g * GROUP + hh)
                gt = gate_ref[0, :, hcols].astype(f32)
                o_ref[0, :, hcols] = (o[hh * t:(hh + 1) * t] * (gt * jax.nn.sigmoid(gt))).astype(o_ref.dtype)


def _attn_sample(tab, mb, q, gate, ck, cv, nk, nv):
    b, t, width = q.shape
    tk = ATTN_TK
    p = ck.shape[1] // N_KV
    nj = p // SAMPLE_TK
    kvw = N_KV * HEAD_DIM
    return pl.pallas_call(
        _attn_sample_kernel,
        grid=(b, nj),
        in_specs=[
            pl.BlockSpec((1, t, width), lambda bb, j: (bb, 0, 0)),
            pl.BlockSpec((1, t, width), lambda bb, j: (bb, 0, 0)),
            pl.BlockSpec((1, SAMPLE_TK * N_KV, HEAD_DIM), lambda bb, j: (bb, j, 0)),
            pl.BlockSpec((1, SAMPLE_TK * N_KV, HEAD_DIM), lambda bb, j: (bb, j, 0)),
            pl.BlockSpec((1, tk, kvw), lambda bb, j: (bb, 0, 0)),
            pl.BlockSpec((1, tk, kvw), lambda bb, j: (bb, 0, 0)),
            pl.BlockSpec((1, t, SAMPLE_TK), lambda bb, j: (bb, 0, j)),
            pl.BlockSpec((1, t, tk), lambda bb, j: (bb, 0, p // tk)),
            pl.BlockSpec((2, N_HEADS, t, tk), lambda bb, j: (0, 0, 0, 0)),
        ],
        out_specs=pl.BlockSpec((1, t, width), lambda bb, j: (bb, 0, 0)),
        out_shape=jax.ShapeDtypeStruct((b, t, width), bf16),
        scratch_shapes=[
            pltpu.VMEM((N_KV, GROUP * t, SAMPLE_TK), f32),
            pltpu.VMEM((N_KV, GROUP * t, SAMPLE_TK), bf16),
            pltpu.VMEM((N_KV, GROUP * t, 1), f32),
            pltpu.VMEM((N_KV, GROUP * t, 1), f32),
            pltpu.VMEM((N_KV, GROUP * t, HEAD_DIM), f32),
        ],
        compiler_params=_cparams(("parallel", "arbitrary")),
        name="attn_sample",
    )(q, gate, ck, cv, nk, nv, mb, mb, tab)


def _pool_kernel(u_ref, hist_ref, gate_ref, gw_ref, scale_ref, o_ref, ext_scr, *, start):
    i = pl.program_id(1)
    tm = u_ref.shape[1]
    gwid = gw_ref.shape[1]

    @pl.when(i == 0)
    def _first():
        ext_scr[0:HALO, :] = hist_ref[0]

    @pl.when(i > 0)
    def _carry():
        ext_scr[0:HALO, :] = ext_scr[tm:tm + HALO, :]

    ext_scr[HALO:HALO + tm, :] = u_ref[0]
    pos = start + i * tm + lax.broadcasted_iota(jnp.int32, (tm, 1), 0)
    for gi, win in enumerate(POOL_WINDOWS):
        lo, hi = gi * gwid, (gi + 1) * gwid
        s = ext_scr[:, lo:hi]
        shift = 1
        while shift < win:
            s = s + pltpu.roll(s, shift, 0)
            shift *= 2
        s = s[HALO:HALO + tm]
        tok = ext_scr[HALO:HALO + tm, lo:hi]
        cnt = jnp.minimum(pos + 1, win).astype(f32)
        mix = (s / cnt - tok).astype(bf16)
        z = jnp.dot(mix, gw_ref[gi], preferred_element_type=f32) * scale_ref[:, lo:hi]
        gt = gate_ref[0, :, lo:hi].astype(f32)
        o_ref[0, :, lo:hi] = (z * (gt * jax.nn.sigmoid(gt))).astype(o_ref.dtype)


def _pool_mix(u, hist16, gate, group_w, scale, start, tm=256):
    b, t, e = u.shape
    tm = min(tm, t)
    gwid = e // N_POOL_GROUPS
    return pl.pallas_call(
        functools.partial(_pool_kernel, start=start),
        grid=(b, t // tm),
        in_specs=[
            pl.BlockSpec((1, tm, e), lambda bb, i: (bb, i, 0)),
            pl.BlockSpec((1, HALO, e), lambda bb, i: (bb, 0, 0)),
            pl.BlockSpec((1, tm, e), lambda bb, i: (bb, i, 0)),
            pl.BlockSpec((N_POOL_GROUPS, gwid, gwid), lambda bb, i: (0, 0, 0)),
            pl.BlockSpec((1, e), lambda bb, i: (0, 0)),
        ],
        out_specs=pl.BlockSpec((1, tm, e), lambda bb, i: (bb, i, 0)),
        out_shape=jax.ShapeDtypeStruct((b, t, e), bf16),
        scratch_shapes=[pltpu.VMEM((HALO + tm, e), f32)],
        compiler_params=_cparams(("parallel", "arbitrary")),
        name="pool_mix",
    )(u, hist16, gate, group_w, scale)


def _attn_weights(w_in):
    qw, kw_ = N_HEADS * HEAD_DIM, N_KV * HEAD_DIM
    o = 0
    parts = {}
    for name, wdt in (("q", qw), ("k", kw_), ("v", kw_), ("qi", H_IDX * D_IDX), ("kiwi", D_IDX + H_IDX),
                      ("gate", qw)):
        parts[name] = w_in[:, o:o + wdt].astype(bf16)
        o += wdt
    parts["kiwi"] = jnp.pad(parts["kiwi"], ((0, 0), (0, 128 - (D_IDX + H_IDX))))
    return parts


def _attn_project(h, wts, b, t):
    q = _matmul(h, wts["q"], bf16).reshape(b, t, -1)
    k = _matmul(h, wts["k"], f32).reshape(b, t, N_KV, HEAD_DIM)
    v = _matmul(h, wts["v"], f32).reshape(b, t, N_KV, HEAD_DIM)
    qi_hm = _matmul_heads(h, wts["qi"], D_IDX, bf16)
    kiwi = _matmul(h, wts["kiwi"], f32).reshape(b, t, 128)
    gate = _matmul(h, wts["gate"], bf16).reshape(b, t, -1)
    ki = kiwi[..., :D_IDX]
    wi = kiwi[..., D_IDX:D_IDX + H_IDX]
    return q, k, v, qi_hm, ki, wi, gate


def _pad_rows(x, rows):
    return jnp.pad(x, ((0, 0), (0, rows - x.shape[1])) + ((0, 0),) * (x.ndim - 2))


def kernel(x_prompt, x_sample, cache_k, cache_v, cache_kidx, state_pool, norm_w, final_norm_w,
           attn_w_in, attn_w_out, rel_bias, pool_w_in, pool_group_w, pool_scale, pool_w_out):
    bp, sp, d = x_prompt.shape
    bs, ts, _ = x_sample.shape
    past = cache_k.shape[2]
    assert ATTN_TQ == ATTN_TK and ATTN_TK >= MAX_DISTANCE and SAMPLE_TK % ATTN_TK == 0
    assert sp % ATTN_TQ == 0 and past % SAMPLE_TK == 0 and past % SELECT_TK == 0
    assert past % CHUNK == 0 and ts <= CHUNK and ts % 16 == 0 and ts >= POOL_HIST
    depth = norm_w.shape[0]

    xp = x_prompt.reshape(bp * sp, d)
    xs = x_sample.reshape(bs * ts, d)
    tab_km = _bias_tables(rel_bias, key_major=True)
    tab_qm = _bias_tables(rel_bias, key_major=False)
    outs = {name: [] for name in ("kp", "vp", "kip", "poolp", "ks", "vs", "kis", "pools")}

    for layer in range(depth):
        hp = _rmsnorm(xp, norm_w[layer], bf16)
        hs = _rmsnorm(xs, norm_w[layer], bf16)
        if layer % 2 == 0:
            a = layer // 2
            wts = _attn_weights(attn_w_in[a])
            w_out = attn_w_out[a].astype(bf16)

            q, k, v, qi_hm, ki, wi, gate = _attn_project(hp, wts, bp, sp)
            nt = sp // ATTN_TK
            k_hm = jnp.transpose(k.astype(bf16), (0, 2, 1, 3)).reshape(bp, N_KV, nt, ATTN_TK, HEAD_DIM)
            vt_hm = jnp.transpose(v.astype(bf16).reshape(bp, nt, ATTN_TK, N_KV, HEAD_DIM), (0, 3, 1, 4, 2))
            og = _attn_prompt(tab_km, qi_hm, jnp.transpose(wi, (0, 2, 1)), q, gate,
                              ki.astype(bf16).reshape(bp, nt, ATTN_TK, D_IDX), k_hm, vt_hm)
            xp = _matmul(og.reshape(bp * sp, -1), w_out, f32, res=xp)
            outs["kp"].append(k); outs["vp"].append(v); outs["kip"].append(ki)

            q, k, v, qi_hm, ki, wi, gate = _attn_project(hs, wts, bs, ts)
            ckit = jnp.transpose(cache_kidx[a].astype(bf16), (0, 2, 1))
            nkit = jnp.transpose(_pad_rows(ki.astype(bf16), ATTN_TK), (0, 2, 1))
            nk = _pad_rows(k.astype(bf16).reshape(bs, ts, -1), ATTN_TK)
            nv = _pad_rows(v.astype(bf16).reshape(bs, ts, -1), ATTN_TK)
            mb = _select_sample(qi_hm, wi, ckit, nkit, ts)
            og = _attn_sample(tab_qm[:, :, :ts], mb, q, gate, cache_k[a].reshape(bs, past * N_KV, HEAD_DIM),
                              cache_v[a].reshape(bs, past * N_KV, HEAD_DIM), nk, nv)
            xs = _matmul(og.reshape(bs * ts, -1), w_out, f32, res=xs)
            outs["ks"].append(k); outs["vs"].append(v); outs["kis"].append(ki)
        else:
            p = layer // 2
            e = pool_w_in.shape[2] // 2
            w_u = pool_w_in[p][:, :e].astype(bf16)
            w_g = pool_w_in[p][:, e:].astype(bf16)
            gw = pool_group_w[p].astype(bf16)
            scale = pool_scale[p].reshape(1, e)
            w_out = pool_w_out[p].astype(bf16)

            u = _matmul(hp, w_u, f32).reshape(bp, sp, e)
            gate = _matmul(hp, w_g, bf16).reshape(bp, sp, e)
            zg = _pool_mix(u, jnp.zeros((bp, HALO, e), f32), gate, gw, scale, 0)
            xp = _matmul(zg.reshape(bp * sp, e), w_out, f32, res=xp)
            outs["poolp"].append(u[:, sp - POOL_HIST:])

            u = _matmul(hs, w_u, f32).reshape(bs, ts, e)
            gate = _matmul(hs, w_g, bf16).reshape(bs, ts, e)
            hist16 = jnp.pad(state_pool[p], ((0, 0), (HALO - POOL_HIST, 0), (0, 0)))
            zg = _pool_mix(u, hist16, gate, gw, scale, past)
            xs = _matmul(zg.reshape(bs * ts, e), w_out, f32, res=xs)
            outs["pools"].append(u[:, ts - POOL_HIST:])

    y_prompt = _rmsnorm(xp, final_norm_w, f32).reshape(bp, sp, d)
    y_sample = _rmsnorm(xs, final_norm_w, f32).reshape(bs, ts, d)
    return (y_prompt, y_sample, jnp.stack(outs["kp"]), jnp.stack(outs["vp"]), jnp.stack(outs["kip"]),
            jnp.stack(outs["poolp"]), jnp.stack(outs["ks"]), jnp.stack(outs["vs"]), jnp.stack(outs["kis"]),
            jnp.stack(outs["pools"]))
```

```python
import functools
import math

import jax
import jax.numpy as jnp
from jax import lax
from jax.experimental import pallas as pl
from jax.experimental.pallas import tpu as pltpu

CHUNK = 64
N_HEADS = 32
HEAD_DIM = 128
N_KV = 8
GROUP = N_HEADS // N_KV
H_IDX = 32
D_IDX = 64
TOPK_MAX = 256
NUM_BUCKETS = 32
MAX_DISTANCE = 128
FAR_BUCKET = NUM_BUCKETS // 2 - 1
POOL_WINDOWS = (2, 4, 8, 16)
N_POOL_GROUPS = len(POOL_WINDOWS)
POOL_HIST = max(POOL_WINDOWS) - 1
EPS = 1e-6

ATTN_TQ = 256
ATTN_TK = 256
SAMPLE_TK = 512
SELECT_TK = 1024
ONES_ROWS = 16
HALO = 16
VMEM_LIMIT = 52 * 1024 * 1024

LANES = 128
f32 = jnp.float32
bf16 = jnp.bfloat16
INT_MIN = -(2 ** 31)
NEG = -1e30
LOG2E = math.log2(math.e)
SM_SCALE_LOG2 = HEAD_DIM ** -0.5 * LOG2E


def _cparams(sem):
    return pltpu.CompilerParams(dimension_semantics=sem, vmem_limit_bytes=VMEM_LIMIT)


def _rmsnorm_kernel(x_ref, w_ref, o_ref):
    x = x_ref[...]
    y = x * lax.rsqrt(jnp.mean(x * x, axis=-1, keepdims=True) + EPS)
    o_ref[...] = (y * w_ref[...]).astype(o_ref.dtype)


def _rmsnorm(x, w, out_dtype, tm=256):
    m, d = x.shape
    tm = min(tm, m)
    assert m % tm == 0
    return pl.pallas_call(
        _rmsnorm_kernel,
        grid=(m // tm,),
        in_specs=[pl.BlockSpec((tm, d), lambda i: (i, 0)), pl.BlockSpec((1, d), lambda i: (0, 0))],
        out_specs=pl.BlockSpec((tm, d), lambda i: (i, 0)),
        out_shape=jax.ShapeDtypeStruct((m, d), out_dtype),
        compiler_params=_cparams(("parallel",)),
        name="rmsnorm",
    )(x, w.reshape(1, d))


def _weight_tile(w_ref):
    w = w_ref[...]
    return w if w.dtype == bf16 else w.astype(bf16)


def _mm_kernel(a_ref, w_ref, o_ref):
    o_ref[...] = jnp.dot(a_ref[...], _weight_tile(w_ref), preferred_element_type=f32).astype(o_ref.dtype)


def _mm_res_kernel(a_ref, w_ref, r_ref, o_ref):
    acc = jnp.dot(a_ref[...], _weight_tile(w_ref), preferred_element_type=f32)
    o_ref[...] = (r_ref[...] + acc).astype(o_ref.dtype)


def _matmul(a, w, out_dtype, res=None, col0=0, n=None, tm=1024, tn=512):
    m, k = a.shape
    n = w.shape[1] - col0 if n is None else n
    tm = min(tm, m)
    tn = min(tn, n)
    assert m % tm == 0 and n % tn == 0 and col0 % tn == 0
    jb = col0 // tn
    in_specs = [pl.BlockSpec((tm, k), lambda i, j: (i, 0)), pl.BlockSpec((k, tn), lambda i, j: (0, j + jb))]
    args = [a, w]
    kern = _mm_kernel
    if res is not None:
        in_specs.append(pl.BlockSpec((tm, tn), lambda i, j: (i, j)))
        args.append(res)
        kern = _mm_res_kernel
    return pl.pallas_call(
        kern,
        grid=(m // tm, n // tn),
        in_specs=in_specs,
        out_specs=pl.BlockSpec((tm, tn), lambda i, j: (i, j)),
        out_shape=jax.ShapeDtypeStruct((m, n), out_dtype),
        compiler_params=_cparams(("parallel", "parallel")),
        name="matmul_res" if res is not None else "matmul",
    )(*args)


def _mm_heads_kernel(a_ref, w_ref, o_ref):
    acc = jnp.dot(a_ref[...], _weight_tile(w_ref), preferred_element_type=f32)
    dh = o_ref.shape[2]
    for hh in range(o_ref.shape[0]):
        o_ref[hh] = acc[:, hh * dh:(hh + 1) * dh].astype(o_ref.dtype)


def _matmul_heads(a, w, col0, n, dh, out_dtype, tm=1024, tn=512):
    m, k = a.shape
    tm = min(tm, m)
    assert m % tm == 0 and n % tn == 0 and tn % dh == 0 and col0 % tn == 0
    jb = col0 // tn
    return pl.pallas_call(
        _mm_heads_kernel,
        grid=(m // tm, n // tn),
        in_specs=[pl.BlockSpec((tm, k), lambda i, j: (i, 0)), pl.BlockSpec((k, tn), lambda i, j: (0, j + jb))],
        out_specs=pl.BlockSpec((tn // dh, tm, dh), lambda i, j: (j, i, 0)),
        out_shape=jax.ShapeDtypeStruct((n // dh, m, dh), out_dtype),
        compiler_params=_cparams(("parallel", "parallel")),
        name="matmul_heads",
    )(a, w)


def _rel_bucket(rel):
    nb = NUM_BUCKETS // 2
    ret = jnp.where(rel > 0, nb, 0)
    n = jnp.abs(rel)
    max_exact = nb // 2
    nf = jnp.maximum(n, 1).astype(f32)
    large = max_exact + (jnp.log(nf / max_exact) / math.log(MAX_DISTANCE / max_exact)
                         * (nb - max_exact)).astype(jnp.int32)
    large = jnp.minimum(large, nb - 1)
    return ret + jnp.where(n < max_exact, n, large)


def _bias_table_kernel(rb_ref, bk_ref, o_ref):
    h = pl.program_id(1)
    bk = bk_ref[0]
    acc = jnp.zeros(bk.shape, f32)
    for b in range(NUM_BUCKETS):
        acc = jnp.where(bk == b, rb_ref[b, h], acc)
    o_ref[0, 0] = (acc - rb_ref[FAR_BUCKET, h]) * LOG2E


def _bias_tables(rel_bias, key_major):
    ax_q, ax_k = (1, 0) if key_major else (0, 1)
    qo = lax.broadcasted_iota(jnp.int32, (ATTN_TK, ATTN_TK), ax_q)
    ko = lax.broadcasted_iota(jnp.int32, (ATTN_TK, ATTN_TK), ax_k)
    buckets = jnp.stack([_rel_bucket(ko - qo), _rel_bucket(ko - ATTN_TK - qo)])
    return pl.pallas_call(
        _bias_table_kernel,
        grid=(2, N_HEADS),
        in_specs=[pl.BlockSpec(memory_space=pltpu.SMEM),
                  pl.BlockSpec((1, ATTN_TK, ATTN_TK), lambda a, h: (a, 0, 0))],
        out_specs=pl.BlockSpec((1, 1, ATTN_TK, ATTN_TK), lambda a, h: (a, h, 0, 0)),
        out_shape=jax.ShapeDtypeStruct((2, N_HEADS, ATTN_TK, ATTN_TK), f32),
        compiler_params=_cparams(("parallel", "parallel")),
        name="bias_tables",
    )(rel_bias, buckets)


def _float_order_key(x):
    bits = pltpu.bitcast(x + 0.0, jnp.int32)
    return bits ^ ((bits >> 31) & jnp.int32(0x7FFFFFFF))


def _score_keys_km(qi_ref, w_t, ki_t):
    tq = qi_ref.shape[1]
    heads_per_dot = 8
    acc = jnp.zeros((ki_t.shape[0], tq), f32)
    for hc in range(H_IDX // heads_per_dot):
        rhs = qi_ref[hc * heads_per_dot:(hc + 1) * heads_per_dot].reshape(heads_per_dot * tq, D_IDX)
        d = lax.dot_general(ki_t, rhs, (((1,), (1,)), ((), ())), preferred_element_type=f32)
        for hh in range(heads_per_dot):
            h = hc * heads_per_dot + hh
            acc = acc + jnp.maximum(d[:, hh * tq:(hh + 1) * tq], 0.0) * w_t[h:h + 1, :]
    return _float_order_key(acc)


def _kth_largest_key_km(key_scr, n_tiles, k_sel):
    tk, tq = key_scr.shape[1], key_scr.shape[2]

    def bit_step(bi, tvec):
        cand = tvec + lax.shift_left(jnp.int32(1), 31 - bi)

        def count_tile(t, c):
            one = jnp.where(key_scr[t] >= cand, 1.0, 0.0)
            return c + jnp.sum(one.reshape(tk // 8, 8, tq), axis=0)

        c = lax.fori_loop(0, n_tiles, count_tile, jnp.zeros((8, tq), f32))
        cnt = jnp.sum(c, axis=0, keepdims=True)
        return jnp.where(cnt >= float(k_sel), cand, tvec)

    return lax.fori_loop(0, 32, bit_step, jnp.full((1, tq), INT_MIN, jnp.int32))


def _softmax_head(s_ref, p_ref, mask_of, pv_prev, m_scr, acc_scr, c0):
    for part in range(pv_prev.shape[1] // LANES):
        cols = slice(c0 + part * LANES, c0 + (part + 1) * LANES)
        lg = s_ref[:, cols] * SM_SCALE_LOG2 + mask_of(part)
        m_prev = m_scr[:, cols]
        m_new = jnp.maximum(m_prev, jnp.max(lg, axis=0, keepdims=True))
        alpha = jnp.exp2(m_prev - m_new)
        p_ref[:, cols] = jnp.exp2((lg - m_new).astype(bf16))
        acc_scr[:, cols] = (acc_scr[:, cols] + pv_prev[:, part * LANES:(part + 1) * LANES]) * alpha
        m_scr[:, cols] = m_new


def _attn_prompt_kernel(qi_ref, wi_ref, q_ref, gate_ref, ki_ref, k_ref, vt_ref, tab_ref, o_ref,
                        key_scr, mb_scr, sa_scr, sb_scr, pa_scr, pb_scr, m_scr, acc_scr, *, k_sel):
    i = pl.program_id(1)
    g = pl.program_id(2)
    tq, tk = ATTN_TQ, ATTN_TK
    masked_tile = mb_scr.shape[0] - 1
    ko = lax.broadcasted_iota(jnp.int32, (tk, tq), 0)
    qo = lax.broadcasted_iota(jnp.int32, (tk, tq), 1)
    causal = (ko // CHUNK) <= (qo // CHUNK)

    @pl.when(g == 0)
    def _select():
        w_t = wi_ref[0] * (D_IDX ** -0.5 * H_IDX ** -0.5)

        def score_tile(t, carry):
            key_scr[t] = _score_keys_km(qi_ref, w_t, ki_ref[0, t])
            return carry

        lax.fori_loop(0, i + 1, score_tile, 0)
        key_scr[i] = jnp.where(causal, key_scr[i], INT_MIN)
        tvec = _kth_largest_key_km(key_scr, i + 1, k_sel)

        def mask_tile(t, carry):
            mb_scr[t] = jnp.where(key_scr[t] >= tvec, 0.0, NEG)
            return carry

        lax.fori_loop(0, i + 1, mask_tile, 0)
        mb_scr[i] = jnp.where(causal, mb_scr[i], NEG)
        mb_scr[masked_tile] = jnp.full((tk, tq), NEG, f32)

    n_far = jnp.maximum(i - 1, 0)
    n_pairs = (n_far + 1) // 2
    t_prev = jnp.maximum(i - 1, 0)
    mask_prev = jnp.where(i >= 1, i - 1, masked_tile)

    m_scr[...] = jnp.full(m_scr.shape, NEG, f32)
    acc_scr[...] = jnp.zeros(acc_scr.shape, f32)
    pb_scr[...] = jnp.zeros(pb_scr.shape, bf16)

    def head_cols(hh):
        return slice(hh * tq, (hh + 1) * tq)

    def qk_head(t, hh):
        q_h = q_ref[0, :, hh * HEAD_DIM:(hh + 1) * HEAD_DIM]
        return lax.dot_general(k_ref[0, 0, t], q_h, (((1,), (1,)), ((), ())), preferred_element_type=f32)

    def pv_head(t, p_ref, hh):
        return jnp.dot(vt_ref[0, 0, t], p_ref[:, head_cols(hh)], preferred_element_type=f32)

    def stage(s_cur, p_cur, mask_t, tab_idx, t_pv, p_pv, t_qk, s_nxt):
        for hh in range(GROUP):
            pv_h = pv_head(t_pv, p_pv, hh)
            if s_nxt is not None:
                s_nxt[:, head_cols(hh)] = qk_head(t_qk, hh)

            def mask_of(part, hh=hh):
                cols = slice(part * LANES, (part + 1) * LANES)
                mask = mb_scr[mask_t, :, cols]
                return mask if tab_idx is None else mask + tab_ref[tab_idx, hh, :, cols]

            _softmax_head(s_cur, p_cur, mask_of, pv_h, m_scr, acc_scr, hh * tq)

    for hh in range(GROUP):
        sa_scr[:, head_cols(hh)] = qk_head(0, hh)

    def far_tile(t):
        return jnp.where(t < n_far, t, masked_tile)

    def pair_step(tau, carry):
        t0 = 2 * tau
        stage(sa_scr, pa_scr, t0, None, jnp.maximum(t0 - 1, 0), pb_scr, jnp.minimum(t0 + 1, n_far - 1), sb_scr)
        stage(sb_scr, pb_scr, far_tile(t0 + 1), None, t0, pa_scr, jnp.minimum(t0 + 2, n_far), sa_scr)
        return carry

    lax.fori_loop(0, n_pairs, pair_step, 0)
    t_last_far = jnp.clip(2 * n_pairs - 1, 0, jnp.maximum(n_far - 1, 0))
    stage(sa_scr, pa_scr, mask_prev, 1, t_last_far, pb_scr, i, sb_scr)
    stage(sb_scr, pb_scr, i, 0, t_prev, pa_scr, None, None)
    for hh in range(GROUP):
        acc = acc_scr[:, head_cols(hh)] + pv_head(i, pb_scr, hh)
        o = (acc[0:HEAD_DIM] / acc[HEAD_DIM:HEAD_DIM + 1]).T
        gt = gate_ref[0, :, hh * HEAD_DIM:(hh + 1) * HEAD_DIM].astype(f32)
        o_ref[0, :, hh * HEAD_DIM:(hh + 1) * HEAD_DIM] = (o * (gt * jax.nn.sigmoid(gt))).astype(o_ref.dtype)


def _attn_prompt(tab, qi_hm, wi_t, q, gate, ki, k_hm, vt_hm):
    b, s, width = q.shape
    tq, tk = ATTN_TQ, ATTN_TK
    nt = s // tk
    nq = s // tq
    k_sel = min(TOPK_MAX, s // 4)
    gw = GROUP * HEAD_DIM
    vrows = HEAD_DIM + ONES_ROWS
    return pl.pallas_call(
        functools.partial(_attn_prompt_kernel, k_sel=k_sel),
        grid=(b, nq, N_KV),
        in_specs=[
            pl.BlockSpec((H_IDX, tq, D_IDX), lambda bb, i, g: (0, bb * nq + i, 0)),
            pl.BlockSpec((1, H_IDX, tq), lambda bb, i, g: (bb, 0, i)),
            pl.BlockSpec((1, tq, gw), lambda bb, i, g: (bb, i, g)),
            pl.BlockSpec((1, tq, gw), lambda bb, i, g: (bb, i, g)),
            pl.BlockSpec((1, nt, tk, D_IDX), lambda bb, i, g: (bb, 0, 0, 0)),
            pl.BlockSpec((1, 1, nt, tk, HEAD_DIM), lambda bb, i, g: (bb, g, 0, 0, 0)),
            pl.BlockSpec((1, 1, nt, vrows, tk), lambda bb, i, g: (bb, g, 0, 0, 0)),
            pl.BlockSpec((2, GROUP, tk, tq), lambda bb, i, g: (0, g, 0, 0)),
        ],
        out_specs=pl.BlockSpec((1, tq, gw), lambda bb, i, g: (bb, i, g)),
        out_shape=jax.ShapeDtypeStruct((b, s, width), bf16),
        scratch_shapes=[
            pltpu.VMEM((nt, tk, tq), jnp.int32),
            pltpu.VMEM((nt + 1, tk, tq), f32),
            pltpu.VMEM((tk, GROUP * tq), f32),
            pltpu.VMEM((tk, GROUP * tq), f32),
            pltpu.VMEM((tk, GROUP * tq), bf16),
            pltpu.VMEM((tk, GROUP * tq), bf16),
            pltpu.VMEM((1, GROUP * tq), f32),
            pltpu.VMEM((vrows, GROUP * tq), f32),
        ],
        compiler_params=_cparams(("parallel", "arbitrary", "arbitrary")),
        name="attn_prompt",
    )(qi_hm, wi_t, q, gate, ki, k_hm, vt_hm, tab)


def _score_keys_qm(qi_ref, w, kt):
    tq = qi_ref.shape[1]
    heads_per_dot = 8
    acc = jnp.zeros((tq, kt.shape[1]), f32)
    for hc in range(H_IDX // heads_per_dot):
        lhs = qi_ref[hc * heads_per_dot:(hc + 1) * heads_per_dot].reshape(heads_per_dot * tq, D_IDX)
        d = jnp.dot(lhs, kt, preferred_element_type=f32)
        for hh in range(heads_per_dot):
            h = hc * heads_per_dot + hh
            acc = acc + jnp.maximum(d[hh * tq:(hh + 1) * tq], 0.0) * w[:, h:h + 1]
    return _float_order_key(acc)


def _kth_largest_key_qm(key_scr, k_sel):
    tq, width = key_scr.shape

    def bit_step(bi, tvec):
        cand = tvec + lax.shift_left(jnp.int32(1), 31 - bi)
        c = jnp.zeros((tq, 128), f32)
        for s in range(width // 128):
            c = c + jnp.where(key_scr[:, s * 128:(s + 1) * 128] >= cand, 1.0, 0.0)
        cnt = jnp.sum(c, axis=1, keepdims=True)
        return jnp.where(cnt >= float(k_sel), cand, tvec)

    return lax.fori_loop(0, 32, bit_step, jnp.full((tq, 1), INT_MIN, jnp.int32))


def _select_sample_kernel(qi_ref, wi_ref, ckit_ref, nkit_ref, mb_ref, key_scr, *, k_sel, n_new):
    tq = qi_ref.shape[1]
    tk = ATTN_TK
    p = ckit_ref.shape[2]
    real = lax.broadcasted_iota(jnp.int32, (tq, tk), 1) < n_new
    w = wi_ref[0] * (D_IDX ** -0.5 * H_IDX ** -0.5)
    for c in range(p // SELECT_TK):
        cols = slice(c * SELECT_TK, (c + 1) * SELECT_TK)
        key_scr[:, cols] = _score_keys_qm(qi_ref, w, ckit_ref[0, :, cols])
    key_scr[:, p:p + tk] = jnp.where(real, _score_keys_qm(qi_ref, w, nkit_ref[0]), INT_MIN)
    tvec = _kth_largest_key_qm(key_scr, k_sel)
    mb_ref[0, :, 0:p] = jnp.where(key_scr[:, 0:p] >= tvec, 0.0, NEG)
    mb_ref[0, :, p:p + tk] = jnp.where(real & (key_scr[:, p:p + tk] >= tvec), 0.0, NEG)


def _select_sample(qi_hm, wi, ckit, nkit, n_new):
    b, t, _ = wi.shape
    p = ckit.shape[2]
    tk = ATTN_TK
    k_sel = min(TOPK_MAX, (p + n_new) // 4)
    return pl.pallas_call(
        functools.partial(_select_sample_kernel, k_sel=k_sel, n_new=n_new),
        grid=(b,),
        in_specs=[
            pl.BlockSpec((H_IDX, t, D_IDX), lambda bb: (0, bb, 0)),
            pl.BlockSpec((1, t, H_IDX), lambda bb: (bb, 0, 0)),
            pl.BlockSpec((1, D_IDX, p), lambda bb: (bb, 0, 0)),
            pl.BlockSpec((1, D_IDX, tk), lambda bb: (bb, 0, 0)),
        ],
        out_specs=pl.BlockSpec((1, t, p + tk), lambda bb: (bb, 0, 0)),
        out_shape=jax.ShapeDtypeStruct((b, t, p + tk), f32),
        scratch_shapes=[pltpu.VMEM((t, p + tk), jnp.int32)],
        compiler_params=_cparams(("parallel",)),
        name="select_sample",
    )(qi_hm, wi, ckit, nkit)


def _attend_tile_qm(q_ref, k_of, v_of, mbt, bias_of, s_scr, p_scr, m_scr, l_scr, acc_scr):
    w = mbt.shape[1]
    for g in range(N_KV):
        qs = jnp.concatenate(
            [q_ref[0, :, (g * GROUP + hh) * HEAD_DIM:(g * GROUP + hh + 1) * HEAD_DIM] for hh in range(GROUP)], axis=0)
        s_scr[g, :, 0:w] = lax.dot_general(qs, k_of(g), (((1,), (1,)), ((), ())), preferred_element_type=f32)
    mb4 = jnp.concatenate([mbt] * GROUP, axis=0)
    for g in range(N_KV):
        lg = s_scr[g, :, 0:w] * SM_SCALE_LOG2 + mb4
        if bias_of is not None:
            lg = lg + bias_of(g)
        m_prev = m_scr[g]
        m_new = jnp.maximum(m_prev, jnp.max(lg, axis=1, keepdims=True))
        alpha = jnp.exp2(m_prev - m_new)
        p = jnp.exp2(lg - m_new)
        l_scr[g] = alpha * l_scr[g] + jnp.sum(p, axis=1, keepdims=True)
        acc_scr[g] = alpha * acc_scr[g]
        m_scr[g] = m_new
        p_scr[g, :, 0:w] = p.astype(bf16)
    for g in range(N_KV):
        acc_scr[g] = acc_scr[g] + jnp.dot(p_scr[g, :, 0:w], v_of(g), preferred_element_type=f32)


def _attn_sample_kernel(q_ref, gate_ref, ck_ref, cv_ref, nk_ref, nv_ref, mbc_ref, mbn_ref, tab_ref, o_ref,
                        s_scr, p_scr, m_scr, l_scr, acc_scr):
    j = pl.program_id(1)
    nj = pl.num_programs(1)
    t = q_ref.shape[1]
    tk = ATTN_TK
    n_keys = mbc_ref.shape[2]

    @pl.when(j == 0)
    def _init():
        m_scr[...] = jnp.full(m_scr.shape, NEG, f32)
        l_scr[...] = jnp.zeros(l_scr.shape, f32)
        acc_scr[...] = jnp.zeros(acc_scr.shape, f32)

    def head_cols(g):
        return slice(g * HEAD_DIM, (g + 1) * HEAD_DIM)

    def cache_head(ref, g):
        return ref[0, pl.ds(g, n_keys, stride=N_KV), :].astype(bf16)

    def cache_tile(last):
        bias_of = None
        if last:
            far = jnp.zeros((t, n_keys - tk), f32)

            def bias_of(g):
                return jnp.concatenate(
                    [jnp.concatenate([far, tab_ref[1, g * GROUP + hh]], axis=1) for hh in range(GROUP)], axis=0)

        _attend_tile_qm(q_ref, functools.partial(cache_head, ck_ref), functools.partial(cache_head, cv_ref),
                        mbc_ref[0], bias_of, s_scr, p_scr, m_scr, l_scr, acc_scr)

    @pl.when(j < nj - 1)
    def _far():
        cache_tile(False)

    @pl.when(j == nj - 1)
    def _near():
        cache_tile(True)
        _attend_tile_qm(q_ref, lambda g: nk_ref[0, :, head_cols(g)], lambda g: nv_ref[0, :, head_cols(g)],
                        mbn_ref[0],
                        lambda g: jnp.concatenate([tab_ref[0, g * GROUP + hh] for hh in range(GROUP)], axis=0),
                        s_scr, p_scr, m_scr, l_scr, acc_scr)
        for g in range(N_KV):
            o = acc_scr[g] / l_scr[g]
            for hh in range(GROUP):
                hcols = head_cols(g * GROUP + hh)
                gt = gate_ref[0, :, hcols].astype(f32)
                o_ref[0, :, hcols] = (o[hh * t:(hh + 1) * t] * (gt * jax.nn.sigmoid(gt))).astype(o_ref.dtype)


def _attn_sample(tab, mb, q, gate, ck, cv, nk, nv):
    b, t, width = q.shape
    tk = ATTN_TK
    p = ck.shape[1] // N_KV
    nj = p // SAMPLE_TK
    kvw = N_KV * HEAD_DIM
    return pl.pallas_call(
        _attn_sample_kernel,
        grid=(b, nj),
        in_specs=[
            pl.BlockSpec((1, t, width), lambda bb, j: (bb, 0, 0)),
            pl.BlockSpec((1, t, width), lambda bb, j: (bb, 0, 0)),
            pl.BlockSpec((1, SAMPLE_TK * N_KV, HEAD_DIM), lambda bb, j: (bb, j, 0)),
            pl.BlockSpec((1, SAMPLE_TK * N_KV, HEAD_DIM), lambda bb, j: (bb, j, 0)),
            pl.BlockSpec((1, tk, kvw), lambda bb, j: (bb, 0, 0)),
            pl.BlockSpec((1, tk, kvw), lambda bb, j: (bb, 0, 0)),
            pl.BlockSpec((1, t, SAMPLE_TK), lambda bb, j: (bb, 0, j)),
            pl.BlockSpec((1, t, tk), lambda bb, j: (bb, 0, p // tk)),
            pl.BlockSpec((2, N_HEADS, t, tk), lambda bb, j: (0, 0, 0, 0)),
        ],
        out_specs=pl.BlockSpec((1, t, width), lambda bb, j: (bb, 0, 0)),
        out_shape=jax.ShapeDtypeStruct((b, t, width), bf16),
        scratch_shapes=[
            pltpu.VMEM((N_KV, GROUP * t, SAMPLE_TK), f32),
            pltpu.VMEM((N_KV, GROUP * t, SAMPLE_TK), bf16),
            pltpu.VMEM((N_KV, GROUP * t, 1), f32),
            pltpu.VMEM((N_KV, GROUP * t, 1), f32),
            pltpu.VMEM((N_KV, GROUP * t, HEAD_DIM), f32),
        ],
        compiler_params=_cparams(("parallel", "arbitrary")),
        name="attn_sample",
    )(q, gate, ck, cv, nk, nv, mb, mb, tab)


def _pool_kernel(u_ref, hist_ref, gate_ref, gw_ref, scale_ref, o_ref, ext_scr, *, start):
    i = pl.program_id(1)
    tm = u_ref.shape[1]
    gwid = gw_ref.shape[1]

    @pl.when(i == 0)
    def _first():
        ext_scr[0:HALO, :] = hist_ref[0]

    @pl.when(i > 0)
    def _carry():
        ext_scr[0:HALO, :] = ext_scr[tm:tm + HALO, :]

    ext_scr[HALO:HALO + tm, :] = u_ref[0]
    pos = start + i * tm + lax.broadcasted_iota(jnp.int32, (tm, 1), 0)
    for gi, win in enumerate(POOL_WINDOWS):
        lo, hi = gi * gwid, (gi + 1) * gwid
        s = ext_scr[:, lo:hi]
        shift = 1
        while shift < win:
            s = s + pltpu.roll(s, shift, 0)
            shift *= 2
        s = s[HALO:HALO + tm]
        tok = ext_scr[HALO:HALO + tm, lo:hi]
        cnt = jnp.minimum(pos + 1, win).astype(f32)
        mix = (s / cnt - tok).astype(bf16)
        z = jnp.dot(mix, gw_ref[gi], preferred_element_type=f32) * scale_ref[:, lo:hi]
        gt = gate_ref[0, :, lo:hi].astype(f32)
        o_ref[0, :, lo:hi] = (z * (gt * jax.nn.sigmoid(gt))).astype(o_ref.dtype)


def _pool_mix(u, hist16, gate, group_w, scale, start, tm=256):
    b, t, e = u.shape
    tm = min(tm, t)
    gwid = e // N_POOL_GROUPS
    return pl.pallas_call(
        functools.partial(_pool_kernel, start=start),
        grid=(b, t // tm),
        in_specs=[
            pl.BlockSpec((1, tm, e), lambda bb, i: (bb, i, 0)),
            pl.BlockSpec((1, HALO, e), lambda bb, i: (bb, 0, 0)),
            pl.BlockSpec((1, tm, e), lambda bb, i: (bb, i, 0)),
            pl.BlockSpec((N_POOL_GROUPS, gwid, gwid), lambda bb, i: (0, 0, 0)),
            pl.BlockSpec((1, e), lambda bb, i: (0, 0)),
        ],
        out_specs=pl.BlockSpec((1, tm, e), lambda bb, i: (bb, i, 0)),
        out_shape=jax.ShapeDtypeStruct((b, t, e), bf16),
        scratch_shapes=[pltpu.VMEM((HALO + tm, e), f32)],
        compiler_params=_cparams(("parallel", "arbitrary")),
        name="pool_mix",
    )(u, hist16, gate, group_w, scale)


Q_COLS = N_HEADS * HEAD_DIM
KV_COLS = N_KV * HEAD_DIM
QI_COLS = H_IDX * D_IDX
OFF_K = Q_COLS
OFF_V = OFF_K + KV_COLS
OFF_QI = OFF_V + KV_COLS
OFF_KIWI = OFF_QI + QI_COLS
OFF_GATE = OFF_KIWI + D_IDX + H_IDX


def _attn_project(h, w_in, b, t):
    w_kiwi = jnp.pad(w_in[:, OFF_KIWI:OFF_GATE].astype(bf16), ((0, 0), (0, 128 - (D_IDX + H_IDX))))
    w_gate = w_in[:, OFF_GATE:].astype(bf16)
    q = _matmul(h, w_in, bf16, col0=0, n=Q_COLS).reshape(b, t, -1)
    k = _matmul(h, w_in, f32, col0=OFF_K, n=KV_COLS).reshape(b, t, N_KV, HEAD_DIM)
    v = _matmul(h, w_in, f32, col0=OFF_V, n=KV_COLS).reshape(b, t, N_KV, HEAD_DIM)
    qi_hm = _matmul_heads(h, w_in, OFF_QI, QI_COLS, D_IDX, bf16)
    kiwi = _matmul(h, w_kiwi, f32).reshape(b, t, 128)
    gate = _matmul(h, w_gate, bf16).reshape(b, t, -1)
    ki = kiwi[..., :D_IDX]
    wi = kiwi[..., D_IDX:D_IDX + H_IDX]
    return q, k, v, qi_hm, ki, wi, gate


def _pad_rows(x, rows):
    return jnp.pad(x, ((0, 0), (0, rows - x.shape[1])) + ((0, 0),) * (x.ndim - 2))


def kernel(x_prompt, x_sample, cache_k, cache_v, cache_kidx, state_pool, norm_w, final_norm_w,
           attn_w_in, attn_w_out, rel_bias, pool_w_in, pool_group_w, pool_scale, pool_w_out):
    bp, sp, d = x_prompt.shape
    bs, ts, _ = x_sample.shape
    past = cache_k.shape[2]
    assert ATTN_TQ == ATTN_TK and ATTN_TK >= MAX_DISTANCE and SAMPLE_TK % ATTN_TK == 0
    assert sp % ATTN_TQ == 0 and past % SAMPLE_TK == 0 and past % SELECT_TK == 0
    assert past % CHUNK == 0 and ts <= CHUNK and ts % 16 == 0 and ts >= POOL_HIST
    depth = norm_w.shape[0]

    xp = x_prompt.reshape(bp * sp, d)
    xs = x_sample.reshape(bs * ts, d)
    tab_km = _bias_tables(rel_bias, key_major=True)
    tab_qm = _bias_tables(rel_bias, key_major=False)
    outs = {name: [] for name in ("kp", "vp", "kip", "poolp", "ks", "vs", "kis", "pools")}

    for layer in range(depth):
        hp = _rmsnorm(xp, norm_w[layer], bf16)
        hs = _rmsnorm(xs, norm_w[layer], bf16)
        if layer % 2 == 0:
            a = layer // 2
            wts = attn_w_in[a]
            w_out = attn_w_out[a]

            q, k, v, qi_hm, ki, wi, gate = _attn_project(hp, wts, bp, sp)
            nt = sp // ATTN_TK
            k_hm = jnp.transpose(k.astype(bf16), (0, 2, 1, 3)).reshape(bp, N_KV, nt, ATTN_TK, HEAD_DIM)
            vt_hm = jnp.transpose(v.astype(bf16).reshape(bp, nt, ATTN_TK, N_KV, HEAD_DIM), (0, 3, 1, 4, 2))
            vt_hm = jnp.concatenate([vt_hm, jnp.ones((bp, N_KV, nt, ONES_ROWS, ATTN_TK), bf16)], axis=3)
            og = _attn_prompt(tab_km, qi_hm, jnp.transpose(wi, (0, 2, 1)), q, gate,
                              ki.astype(bf16).reshape(bp, nt, ATTN_TK, D_IDX), k_hm, vt_hm)
            xp = _matmul(og.reshape(bp * sp, -1), w_out, f32, res=xp)
            outs["kp"].append(k); outs["vp"].append(v); outs["kip"].append(ki)

            q, k, v, qi_hm, ki, wi, gate = _attn_project(hs, wts, bs, ts)
            ckit = jnp.transpose(cache_kidx[a].astype(bf16), (0, 2, 1))
            nkit = jnp.transpose(_pad_rows(ki.astype(bf16), ATTN_TK), (0, 2, 1))
            nk = _pad_rows(k.astype(bf16).reshape(bs, ts, -1), ATTN_TK)
            nv = _pad_rows(v.astype(bf16).reshape(bs, ts, -1), ATTN_TK)
            mb = _select_sample(qi_hm, wi, ckit, nkit, ts)
            og = _attn_sample(tab_qm[:, :, :ts], mb, q, gate, cache_k[a].reshape(bs, past * N_KV, HEAD_DIM),
                              cache_v[a].reshape(bs, past * N_KV, HEAD_DIM), nk, nv)
            xs = _matmul(og.reshape(bs * ts, -1), w_out, f32, res=xs)
            outs["ks"].append(k); outs["vs"].append(v); outs["kis"].append(ki)
        else:
            p = layer // 2
            e = pool_w_in.shape[2] // 2
            w_in = pool_w_in[p]
            gw = pool_group_w[p].astype(bf16)
            scale = pool_scale[p].reshape(1, e)
            w_out = pool_w_out[p]

            u = _matmul(hp, w_in, f32, col0=0, n=e).reshape(bp, sp, e)
            gate = _matmul(hp, w_in, bf16, col0=e, n=e).reshape(bp, sp, e)
            zg = _pool_mix(u, jnp.zeros((bp, HALO, e), f32), gate, gw, scale, 0)
            xp = _matmul(zg.reshape(bp * sp, e), w_out, f32, res=xp)
            outs["poolp"].append(u[:, sp - POOL_HIST:])

            u = _matmul(hs, w_in, f32, col0=0, n=e).reshape(bs, ts, e)
            gate = _matmul(hs, w_in, bf16, col0=e, n=e).reshape(bs, ts, e)
            hist16 = jnp.pad(state_pool[p], ((0, 0), (HALO - POOL_HIST, 0), (0, 0)))
            zg = _pool_mix(u, hist16, gate, gw, scale, past)
            xs = _matmul(zg.reshape(bs * ts, e), w_out, f32, res=xs)
            outs["pools"].append(u[:, ts - POOL_HIST:])

    y_prompt = _rmsnorm(xp, final_norm_w, f32).reshape(bp, sp, d)
    y_sample = _rmsnorm(xs, final_norm_w, f32).reshape(bs, ts, d)
    return (y_prompt, y_sample, jnp.stack(outs["kp"]), jnp.stack(outs["vp"]), jnp.stack(outs["kip"]),
            jnp.stack(outs["poolp"]), jnp.stack(outs["ks"]), jnp.stack(outs["vs"]), jnp.stack(outs["kis"]),
            jnp.stack(outs["pools"]))
```

```python
import functools
import math

import jax
import jax.numpy as jnp
from jax import lax
from jax.experimental import pallas as pl
from jax.experimental.pallas import tpu as pltpu

CHUNK = 64
N_HEADS = 32
HEAD_DIM = 128
N_KV = 8
GROUP = N_HEADS // N_KV
H_IDX = 32
D_IDX = 64
TOPK_MAX = 256
NUM_BUCKETS = 32
MAX_DISTANCE = 128
FAR_BUCKET = NUM_BUCKETS // 2 - 1
POOL_WINDOWS = (2, 4, 8, 16)
N_POOL_GROUPS = len(POOL_WINDOWS)
POOL_HIST = max(POOL_WINDOWS) - 1
EPS = 1e-6

ATTN_TQ = 256
ATTN_TK = 256
SAMPLE_TK = 512
SELECT_TK = 1024
ONES_ROWS = 16
HALO = 16
VMEM_LIMIT = 52 * 1024 * 1024

f32 = jnp.float32
bf16 = jnp.bfloat16
INT_MIN = -(2 ** 31)
NEG = -1e30
LOG2E = math.log2(math.e)
SM_SCALE_LOG2 = HEAD_DIM ** -0.5 * LOG2E


def _cparams(sem):
    return pltpu.CompilerParams(dimension_semantics=sem, vmem_limit_bytes=VMEM_LIMIT)


def _rmsnorm_kernel(x_ref, w_ref, o_ref):
    x = x_ref[...]
    y = x * lax.rsqrt(jnp.mean(x * x, axis=-1, keepdims=True) + EPS)
    o_ref[...] = (y * w_ref[...]).astype(o_ref.dtype)


def _rmsnorm(x, w, out_dtype, tm=256):
    m, d = x.shape
    tm = min(tm, m)
    assert m % tm == 0
    return pl.pallas_call(
        _rmsnorm_kernel,
        grid=(m // tm,),
        in_specs=[pl.BlockSpec((tm, d), lambda i: (i, 0)), pl.BlockSpec((1, d), lambda i: (0, 0))],
        out_specs=pl.BlockSpec((tm, d), lambda i: (i, 0)),
        out_shape=jax.ShapeDtypeStruct((m, d), out_dtype),
        compiler_params=_cparams(("parallel",)),
        name="rmsnorm",
    )(x, w.reshape(1, d))


def _mm_kernel(a_ref, w_ref, o_ref, *, scale):
    acc = jnp.dot(a_ref[...], w_ref[...], preferred_element_type=f32)
    if scale is not None:
        acc = acc * scale
    o_ref[...] = acc.astype(o_ref.dtype)


def _mm_res_kernel(a_ref, w_ref, r_ref, o_ref):
    acc = jnp.dot(a_ref[...], w_ref[...], preferred_element_type=f32)
    o_ref[...] = (r_ref[...] + acc).astype(o_ref.dtype)


def _matmul(a, w, out_dtype, res=None, scale=None, tm=1024, tn=512):
    m, k = a.shape
    n = w.shape[1]
    tm = min(tm, m)
    tn = min(tn, n)
    assert m % tm == 0 and n % tn == 0
    in_specs = [pl.BlockSpec((tm, k), lambda i, j: (i, 0)), pl.BlockSpec((k, tn), lambda i, j: (0, j))]
    args = [a, w]
    kern = functools.partial(_mm_kernel, scale=scale)
    if res is not None:
        assert scale is None
        in_specs.append(pl.BlockSpec((tm, tn), lambda i, j: (i, j)))
        args.append(res)
        kern = _mm_res_kernel
    return pl.pallas_call(
        kern,
        grid=(m // tm, n // tn),
        in_specs=in_specs,
        out_specs=pl.BlockSpec((tm, tn), lambda i, j: (i, j)),
        out_shape=jax.ShapeDtypeStruct((m, n), out_dtype),
        compiler_params=_cparams(("parallel", "parallel")),
        name="matmul_res" if res is not None else "matmul",
    )(*args)


def _mm_heads_kernel(a_ref, w_ref, o_ref):
    acc = jnp.dot(a_ref[...], w_ref[...], preferred_element_type=f32)
    dh = o_ref.shape[2]
    for hh in range(o_ref.shape[0]):
        o_ref[hh] = acc[:, hh * dh:(hh + 1) * dh].astype(o_ref.dtype)


def _matmul_heads(a, w, dh, out_dtype, tm=1024, tn=512):
    m, k = a.shape
    n = w.shape[1]
    tm = min(tm, m)
    assert m % tm == 0 and n % tn == 0 and tn % dh == 0
    return pl.pallas_call(
        _mm_heads_kernel,
        grid=(m // tm, n // tn),
        in_specs=[pl.BlockSpec((tm, k), lambda i, j: (i, 0)), pl.BlockSpec((k, tn), lambda i, j: (0, j))],
        out_specs=pl.BlockSpec((tn // dh, tm, dh), lambda i, j: (j, i, 0)),
        out_shape=jax.ShapeDtypeStruct((n // dh, m, dh), out_dtype),
        compiler_params=_cparams(("parallel", "parallel")),
        name="matmul_heads",
    )(a, w)


def _rel_bucket(rel):
    nb = NUM_BUCKETS // 2
    ret = jnp.where(rel > 0, nb, 0)
    n = jnp.abs(rel)
    max_exact = nb // 2
    nf = jnp.maximum(n, 1).astype(f32)
    large = max_exact + (jnp.log(nf / max_exact) / math.log(MAX_DISTANCE / max_exact)
                         * (nb - max_exact)).astype(jnp.int32)
    large = jnp.minimum(large, nb - 1)
    return ret + jnp.where(n < max_exact, n, large)


def _bias_table_kernel(rb_ref, bk_ref, o_ref):
    h = pl.program_id(1)
    bk = bk_ref[0]
    acc = jnp.zeros(bk.shape, f32)
    for b in range(NUM_BUCKETS):
        acc = jnp.where(bk == b, rb_ref[b, h], acc)
    o_ref[0, 0] = (acc - rb_ref[FAR_BUCKET, h]) * LOG2E


def _bias_tables(rel_bias, key_major):
    ax_q, ax_k = (1, 0) if key_major else (0, 1)
    qo = lax.broadcasted_iota(jnp.int32, (ATTN_TK, ATTN_TK), ax_q)
    ko = lax.broadcasted_iota(jnp.int32, (ATTN_TK, ATTN_TK), ax_k)
    buckets = jnp.stack([_rel_bucket(ko - qo), _rel_bucket(ko - ATTN_TK - qo)])
    return pl.pallas_call(
        _bias_table_kernel,
        grid=(2, N_HEADS),
        in_specs=[pl.BlockSpec(memory_space=pltpu.SMEM),
                  pl.BlockSpec((1, ATTN_TK, ATTN_TK), lambda a, h: (a, 0, 0))],
        out_specs=pl.BlockSpec((1, 1, ATTN_TK, ATTN_TK), lambda a, h: (a, h, 0, 0)),
        out_shape=jax.ShapeDtypeStruct((2, N_HEADS, ATTN_TK, ATTN_TK), f32),
        compiler_params=_cparams(("parallel", "parallel")),
        name="bias_tables",
    )(rel_bias, buckets)


def _float_order_key(x):
    bits = pltpu.bitcast(x + 0.0, jnp.int32)
    return bits ^ ((bits >> 31) & jnp.int32(0x7FFFFFFF))


def _score_keys_km(qi_ref, w_t, ki_t):
    tq = qi_ref.shape[1]
    heads_per_dot = 8
    acc = jnp.zeros((ki_t.shape[0], tq), f32)
    for hc in range(H_IDX // heads_per_dot):
        rhs = qi_ref[hc * heads_per_dot:(hc + 1) * heads_per_dot].reshape(heads_per_dot * tq, D_IDX)
        d = lax.dot_general(ki_t, rhs, (((1,), (1,)), ((), ())), preferred_element_type=f32)
        for hh in range(heads_per_dot):
            h = hc * heads_per_dot + hh
            acc = acc + jnp.maximum(d[:, hh * tq:(hh + 1) * tq], 0.0) * w_t[h:h + 1, :]
    return _float_order_key(acc)


def _kth_largest_key_km(key_scr, n_tiles, k_sel):
    tk, tq = key_scr.shape[1], key_scr.shape[2]

    def bit_step(bi, tvec):
        cand = tvec + lax.shift_left(jnp.int32(1), 31 - bi)

        def count_tile(t, c):
            one = jnp.where(key_scr[t] >= cand, 1.0, 0.0)
            return c + jnp.sum(one.reshape(tk // 8, 8, tq), axis=0)

        c = lax.fori_loop(0, n_tiles, count_tile, jnp.zeros((8, tq), f32))
        cnt = jnp.sum(c, axis=0, keepdims=True)
        return jnp.where(cnt >= float(k_sel), cand, tvec)

    return lax.fori_loop(0, 32, bit_step, jnp.full((1, tq), INT_MIN, jnp.int32))


def _softmax_stage(s_ref, p_ref, mask_bias, pv_prev, m_scr, acc_scr):
    lg = s_ref[...] + mask_bias
    m_prev = m_scr[...]
    m_new = jnp.maximum(m_prev, jnp.max(lg, axis=0, keepdims=True))
    alpha = jnp.exp2(m_prev - m_new)
    p_ref[...] = jnp.exp2(lg - m_new).astype(bf16)
    acc_scr[...] = (acc_scr[...] + pv_prev) * alpha
    m_scr[...] = m_new


def _attn_prompt_kernel(qi_ref, wi_ref, q_ref, gate_ref, ki_ref, k_ref, vt_ref, tab_ref, o_ref,
                        key_scr, mb_scr, nb_scr, sa_scr, sb_scr, pa_scr, pb_scr, m_scr, acc_scr, *, k_sel):
    i = pl.program_id(1)
    g = pl.program_id(2)
    tq, tk = ATTN_TQ, ATTN_TK
    ko = lax.broadcasted_iota(jnp.int32, (tk, tq), 0)
    qo = lax.broadcasted_iota(jnp.int32, (tk, tq), 1)
    causal = (ko // CHUNK) <= (qo // CHUNK)

    @pl.when(g == 0)
    def _select():
        w_t = wi_ref[0] * (D_IDX ** -0.5 * H_IDX ** -0.5)

        def score_tile(t, carry):
            key_scr[t] = _score_keys_km(qi_ref, w_t, ki_ref[0, t])
            return carry

        lax.fori_loop(0, i + 1, score_tile, 0)
        key_scr[i] = jnp.where(causal, key_scr[i], INT_MIN)
        tvec = _kth_largest_key_km(key_scr, i + 1, k_sel)

        def mask_tile(t, carry):
            mb_scr[t] = jnp.where(key_scr[t] >= tvec, 0.0, NEG)
            return carry

        lax.fori_loop(0, i + 1, mask_tile, 0)
        mb_scr[i] = jnp.where(causal, mb_scr[i], NEG)

    n_far = jnp.maximum(i - 1, 0)
    n_pairs = (n_far + 1) // 2
    t_prev = jnp.maximum(i - 1, 0)
    mb_prev = jnp.where(i >= 1, mb_scr[t_prev], NEG)
    mb_diag = mb_scr[i]
    for hh in range(GROUP):
        nb_scr[0, :, hh * tq:(hh + 1) * tq] = mb_prev + tab_ref[1, hh]
        nb_scr[1, :, hh * tq:(hh + 1) * tq] = mb_diag + tab_ref[0, hh]
    m_scr[...] = jnp.full(m_scr.shape, NEG, f32)
    acc_scr[...] = jnp.zeros(acc_scr.shape, f32)
    pb_scr[...] = jnp.zeros(pb_scr.shape, bf16)
    qs = jnp.concatenate([q_ref[0, :, hh * HEAD_DIM:(hh + 1) * HEAD_DIM] for hh in range(GROUP)], axis=0)

    def qk(t):
        return lax.dot_general(k_ref[0, 0, t], qs, (((1,), (1,)), ((), ())), preferred_element_type=f32)

    def pv(t, p_ref):
        return jnp.dot(vt_ref[0, 0, t], p_ref[...], preferred_element_type=f32)

    def far_mask(t):
        mbt = jnp.where(t < n_far, mb_scr[jnp.minimum(t, n_far - 1)], NEG)
        return jnp.concatenate([mbt] * GROUP, axis=1)

    sa_scr[...] = qk(0)

    def pair_step(tau, carry):
        t0 = 2 * tau
        pv_b = pv(jnp.maximum(t0 - 1, 0), pb_scr)
        sb_scr[...] = qk(jnp.minimum(t0 + 1, n_far - 1))
        _softmax_stage(sa_scr, pa_scr, far_mask(t0), pv_b, m_scr, acc_scr)
        pv_a = pv(t0, pa_scr)
        sa_scr[...] = qk(jnp.minimum(t0 + 2, n_far))
        _softmax_stage(sb_scr, pb_scr, far_mask(t0 + 1), pv_a, m_scr, acc_scr)
        return carry

    lax.fori_loop(0, n_pairs, pair_step, 0)
    pv_b = pv(jnp.clip(2 * n_pairs - 1, 0, jnp.maximum(n_far - 1, 0)), pb_scr)
    sb_scr[...] = qk(i)
    _softmax_stage(sa_scr, pa_scr, nb_scr[0], pv_b, m_scr, acc_scr)
    pv_a = pv(t_prev, pa_scr)
    _softmax_stage(sb_scr, pb_scr, nb_scr[1], pv_a, m_scr, acc_scr)
    acc = acc_scr[...] + pv(i, pb_scr)
    o_t = acc[0:HEAD_DIM] / acc[HEAD_DIM:HEAD_DIM + 1]
    for hh in range(GROUP):
        o = o_t[:, hh * tq:(hh + 1) * tq].T
        gt = gate_ref[0, :, hh * HEAD_DIM:(hh + 1) * HEAD_DIM].astype(f32)
        o_ref[0, :, hh * HEAD_DIM:(hh + 1) * HEAD_DIM] = (o * (gt * jax.nn.sigmoid(gt))).astype(o_ref.dtype)


def _attn_prompt(tab, qi_hm, wi_t, q, gate, ki, k_hm, vt_hm):
    b, s, width = q.shape
    tq, tk = ATTN_TQ, ATTN_TK
    nt = s // tk
    nq = s // tq
    k_sel = min(TOPK_MAX, s // 4)
    gw = GROUP * HEAD_DIM
    vrows = HEAD_DIM + ONES_ROWS
    return pl.pallas_call(
        functools.partial(_attn_prompt_kernel, k_sel=k_sel),
        grid=(b, nq, N_KV),
        in_specs=[
            pl.BlockSpec((H_IDX, tq, D_IDX), lambda bb, i, g: (0, bb * nq + i, 0)),
            pl.BlockSpec((1, H_IDX, tq), lambda bb, i, g: (bb, 0, i)),
            pl.BlockSpec((1, tq, gw), lambda bb, i, g: (bb, i, g)),
            pl.BlockSpec((1, tq, gw), lambda bb, i, g: (bb, i, g)),
            pl.BlockSpec((1, nt, tk, D_IDX), lambda bb, i, g: (bb, 0, 0, 0)),
            pl.BlockSpec((1, 1, nt, tk, HEAD_DIM), lambda bb, i, g: (bb, g, 0, 0, 0)),
            pl.BlockSpec((1, 1, nt, vrows, tk), lambda bb, i, g: (bb, g, 0, 0, 0)),
            pl.BlockSpec((2, GROUP, tk, tq), lambda bb, i, g: (0, g, 0, 0)),
        ],
        out_specs=pl.BlockSpec((1, tq, gw), lambda bb, i, g: (bb, i, g)),
        out_shape=jax.ShapeDtypeStruct((b, s, width), bf16),
        scratch_shapes=[
            pltpu.VMEM((nt, tk, tq), jnp.int32),
            pltpu.VMEM((nt, tk, tq), f32),
            pltpu.VMEM((2, tk, GROUP * tq), f32),
            pltpu.VMEM((tk, GROUP * tq), f32),
            pltpu.VMEM((tk, GROUP * tq), f32),
            pltpu.VMEM((tk, GROUP * tq), bf16),
            pltpu.VMEM((tk, GROUP * tq), bf16),
            pltpu.VMEM((1, GROUP * tq), f32),
            pltpu.VMEM((vrows, GROUP * tq), f32),
        ],
        compiler_params=_cparams(("parallel", "arbitrary", "arbitrary")),
        name="attn_prompt",
    )(qi_hm, wi_t, q, gate, ki, k_hm, vt_hm, tab)


def _score_keys_qm(qi_ref, w, kt):
    tq = qi_ref.shape[1]
    heads_per_dot = 8
    acc = jnp.zeros((tq, kt.shape[1]), f32)
    for hc in range(H_IDX // heads_per_dot):
        lhs = qi_ref[hc * heads_per_dot:(hc + 1) * heads_per_dot].reshape(heads_per_dot * tq, D_IDX)
        d = jnp.dot(lhs, kt, preferred_element_type=f32)
        for hh in range(heads_per_dot):
            h = hc * heads_per_dot + hh
            acc = acc + jnp.maximum(d[hh * tq:(hh + 1) * tq], 0.0) * w[:, h:h + 1]
    return _float_order_key(acc)


def _kth_largest_key_qm(key_scr, k_sel):
    tq, width = key_scr.shape

    def bit_step(bi, tvec):
        cand = tvec + lax.shift_left(jnp.int32(1), 31 - bi)
        c = jnp.zeros((tq, 128), f32)
        for s in range(width // 128):
            c = c + jnp.where(key_scr[:, s * 128:(s + 1) * 128] >= cand, 1.0, 0.0)
        cnt = jnp.sum(c, axis=1, keepdims=True)
        return jnp.where(cnt >= float(k_sel), cand, tvec)

    return lax.fori_loop(0, 32, bit_step, jnp.full((tq, 1), INT_MIN, jnp.int32))


def _select_sample_kernel(qi_ref, wi_ref, ckit_ref, nkit_ref, mb_ref, key_scr, *, k_sel, n_new):
    tq = qi_ref.shape[1]
    tk = ATTN_TK
    p = ckit_ref.shape[2]
    real = lax.broadcasted_iota(jnp.int32, (tq, tk), 1) < n_new
    w = wi_ref[0] * (D_IDX ** -0.5 * H_IDX ** -0.5)
    for c in range(p // SELECT_TK):
        cols = slice(c * SELECT_TK, (c + 1) * SELECT_TK)
        key_scr[:, cols] = _score_keys_qm(qi_ref, w, ckit_ref[0, :, cols])
    key_scr[:, p:p + tk] = jnp.where(real, _score_keys_qm(qi_ref, w, nkit_ref[0]), INT_MIN)
    tvec = _kth_largest_key_qm(key_scr, k_sel)
    mb_ref[0, :, 0:p] = jnp.where(key_scr[:, 0:p] >= tvec, 0.0, NEG)
    mb_ref[0, :, p:p + tk] = jnp.where(real & (key_scr[:, p:p + tk] >= tvec), 0.0, NEG)


def _select_sample(qi_hm, wi, ckit, nkit, n_new):
    b, t, _ = wi.shape
    p = ckit.shape[2]
    tk = ATTN_TK
    k_sel = min(TOPK_MAX, (p + n_new) // 4)
    return pl.pallas_call(
        functools.partial(_select_sample_kernel, k_sel=k_sel, n_new=n_new),
        grid=(b,),
        in_specs=[
            pl.BlockSpec((H_IDX, t, D_IDX), lambda bb: (0, bb, 0)),
            pl.BlockSpec((1, t, H_IDX), lambda bb: (bb, 0, 0)),
            pl.BlockSpec((1, D_IDX, p), lambda bb: (bb, 0, 0)),
            pl.BlockSpec((1, D_IDX, tk), lambda bb: (bb, 0, 0)),
        ],
        out_specs=pl.BlockSpec((1, t, p + tk), lambda bb: (bb, 0, 0)),
        out_shape=jax.ShapeDtypeStruct((b, t, p + tk), f32),
        scratch_shapes=[pltpu.VMEM((t, p + tk), jnp.int32)],
        compiler_params=_cparams(("parallel",)),
        name="select_sample",
    )(qi_hm, wi, ckit, nkit)


def _attend_tile_qm(q_ref, k_of, v_of, mbt, bias_of, s_scr, p_scr, m_scr, l_scr, acc_scr):
    w = mbt.shape[1]
    mb4 = jnp.concatenate([mbt] * GROUP, axis=0)
    for g in range(N_KV):
        qs = jnp.concatenate(
            [q_ref[0, :, (g * GROUP + hh) * HEAD_DIM:(g * GROUP + hh + 1) * HEAD_DIM] for hh in range(GROUP)], axis=0)
        s_scr[g, :, 0:w] = lax.dot_general(qs, k_of(g), (((1,), (1,)), ((), ())), preferred_element_type=f32)
    for g in range(N_KV):
        lg = s_scr[g, :, 0:w] + mb4
        if bias_of is not None:
            lg = lg + bias_of(g)
        m_prev = m_scr[g]
        m_new = jnp.maximum(m_prev, jnp.max(lg, axis=1, keepdims=True))
        alpha = jnp.exp2(m_prev - m_new)
        p = jnp.exp2(lg - m_new)
        l_scr[g] = alpha * l_scr[g] + jnp.sum(p, axis=1, keepdims=True)
        acc_scr[g] = alpha * acc_scr[g]
        m_scr[g] = m_new
        p_scr[g, :, 0:w] = p.astype(bf16)
    for g in range(N_KV):
        acc_scr[g] = acc_scr[g] + jnp.dot(p_scr[g, :, 0:w], v_of(g), preferred_element_type=f32)


def _attn_sample_kernel(q_ref, gate_ref, ck_ref, cv_ref, nk_ref, nv_ref, mbc_ref, mbn_ref, tab_ref, o_ref,
                        s_scr, p_scr, m_scr, l_scr, acc_scr):
    j = pl.program_id(1)
    nj = pl.num_programs(1)
    t = q_ref.shape[1]
    tk = ATTN_TK
    n_keys = mbc_ref.shape[2]

    @pl.when(j == 0)
    def _init():
        m_scr[...] = jnp.full(m_scr.shape, NEG, f32)
        l_scr[...] = jnp.zeros(l_scr.shape, f32)
        acc_scr[...] = jnp.zeros(acc_scr.shape, f32)

    def head_cols(g):
        return slice(g * HEAD_DIM, (g + 1) * HEAD_DIM)

    def cache_head(ref, g):
        return ref[0, pl.ds(g, n_keys, stride=N_KV), :].astype(bf16)

    def cache_tile(last):
        bias_of = None
        if last:
            far = jnp.zeros((t, n_keys - tk), f32)

            def bias_of(g):
                return jnp.concatenate(
                    [jnp.concatenate([far, tab_ref[1, g * GROUP + hh]], axis=1) for hh in range(GROUP)], axis=0)

        _attend_tile_qm(q_ref, functools.partial(cache_head, ck_ref), functools.partial(cache_head, cv_ref),
                        mbc_ref[0], bias_of, s_scr, p_scr, m_scr, l_scr, acc_scr)

    @pl.when(j < nj - 1)
    def _far():
        cache_tile(False)

    @pl.when(j == nj - 1)
    def _near():
        cache_tile(True)
        _attend_tile_qm(q_ref, lambda g: nk_ref[0, :, head_cols(g)], lambda g: nv_ref[0, :, head_cols(g)],
                        mbn_ref[0],
                        lambda g: jnp.concatenate([tab_ref[0, g * GROUP + hh] for hh in range(GROUP)], axis=0),
                        s_scr, p_scr, m_scr, l_scr, acc_scr)
        for g in range(N_KV):
            o = acc_scr[g] / l_scr[g]
            for hh in range(GROUP):
                hcols = head_cols(g * GROUP + hh)
                gt = gate_ref[0, :, hcols].astype(f32)
                o_ref[0, :, hcols] = (o[hh * t:(hh + 1) * t] * (gt * jax.nn.sigmoid(gt))).astype(o_ref.dtype)


def _attn_sample(tab, mb, q, gate, ck, cv, nk, nv):
    b, t, width = q.shape
    tk = ATTN_TK
    p = ck.shape[1] // N_KV
    nj = p // SAMPLE_TK
    kvw = N_KV * HEAD_DIM
    return pl.pallas_call(
        _attn_sample_kernel,
        grid=(b, nj),
        in_specs=[
            pl.BlockSpec((1, t, width), lambda bb, j: (bb, 0, 0)),
            pl.BlockSpec((1, t, width), lambda bb, j: (bb, 0, 0)),
            pl.BlockSpec((1, SAMPLE_TK * N_KV, HEAD_DIM), lambda bb, j: (bb, j, 0)),
            pl.BlockSpec((1, SAMPLE_TK * N_KV, HEAD_DIM), lambda bb, j: (bb, j, 0)),
            pl.BlockSpec((1, tk, kvw), lambda bb, j: (bb, 0, 0)),
            pl.BlockSpec((1, tk, kvw), lambda bb, j: (bb, 0, 0)),
            pl.BlockSpec((1, t, SAMPLE_TK), lambda bb, j: (bb, 0, j)),
            pl.BlockSpec((1, t, tk), lambda bb, j: (bb, 0, p // tk)),
            pl.BlockSpec((2, N_HEADS, t, tk), lambda bb, j: (0, 0, 0, 0)),
        ],
        out_specs=pl.BlockSpec((1, t, width), lambda bb, j: (bb, 0, 0)),
        out_shape=jax.ShapeDtypeStruct((b, t, width), bf16),
        scratch_shapes=[
            pltpu.VMEM((N_KV, GROUP * t, SAMPLE_TK), f32),
            pltpu.VMEM((N_KV, GROUP * t, SAMPLE_TK), bf16),
            pltpu.VMEM((N_KV, GROUP * t, 1), f32),
            pltpu.VMEM((N_KV, GROUP * t, 1), f32),
            pltpu.VMEM((N_KV, GROUP * t, HEAD_DIM), f32),
        ],
        compiler_params=_cparams(("parallel", "arbitrary")),
        name="attn_sample",
    )(q, gate, ck, cv, nk, nv, mb, mb, tab)


def _pool_kernel(u_ref, hist_ref, gate_ref, gw_ref, scale_ref, o_ref, ext_scr, *, start):
    i = pl.program_id(1)
    tm = u_ref.shape[1]
    gwid = gw_ref.shape[1]

    @pl.when(i == 0)
    def _first():
        ext_scr[0:HALO, :] = hist_ref[0]

    @pl.when(i > 0)
    def _carry():
        ext_scr[0:HALO, :] = ext_scr[tm:tm + HALO, :]

    ext_scr[HALO:HALO + tm, :] = u_ref[0]
    pos = start + i * tm + lax.broadcasted_iota(jnp.int32, (tm, 1), 0)
    for gi, win in enumerate(POOL_WINDOWS):
        lo, hi = gi * gwid, (gi + 1) * gwid
        s = ext_scr[:, lo:hi]
        shift = 1
        while shift < win:
            s = s + pltpu.roll(s, shift, 0)
            shift *= 2
        s = s[HALO:HALO + tm]
        tok = ext_scr[HALO:HALO + tm, lo:hi]
        cnt = jnp.minimum(pos + 1, win).astype(f32)
        mix = (s / cnt - tok).astype(bf16)
        z = jnp.dot(mix, gw_ref[gi], preferred_element_type=f32) * scale_ref[:, lo:hi]
        gt = gate_ref[0, :, lo:hi].astype(f32)
        o_ref[0, :, lo:hi] = (z * (gt * jax.nn.sigmoid(gt))).astype(o_ref.dtype)


def _pool_mix(u, hist16, gate, group_w, scale, start, tm=256):
    b, t, e = u.shape
    tm = min(tm, t)
    gwid = e // N_POOL_GROUPS
    return pl.pallas_call(
        functools.partial(_pool_kernel, start=start),
        grid=(b, t // tm),
        in_specs=[
            pl.BlockSpec((1, tm, e), lambda bb, i: (bb, i, 0)),
            pl.BlockSpec((1, HALO, e), lambda bb, i: (bb, 0, 0)),
            pl.BlockSpec((1, tm, e), lambda bb, i: (bb, i, 0)),
            pl.BlockSpec((N_POOL_GROUPS, gwid, gwid), lambda bb, i: (0, 0, 0)),
            pl.BlockSpec((1, e), lambda bb, i: (0, 0)),
        ],
        out_specs=pl.BlockSpec((1, tm, e), lambda bb, i: (bb, i, 0)),
        out_shape=jax.ShapeDtypeStruct((b, t, e), bf16),
        scratch_shapes=[pltpu.VMEM((HALO + tm, e), f32)],
        compiler_params=_cparams(("parallel", "arbitrary")),
        name="pool_mix",
    )(u, hist16, gate, group_w, scale)


def _attn_weights(w_in):
    qw, kw_ = N_HEADS * HEAD_DIM, N_KV * HEAD_DIM
    o = 0
    parts = {}
    for name, wdt in (("q", qw), ("k", kw_), ("v", kw_), ("qi", H_IDX * D_IDX), ("kiwi", D_IDX + H_IDX),
                      ("gate", qw)):
        parts[name] = w_in[:, o:o + wdt].astype(bf16)
        o += wdt
    parts["kiwi"] = jnp.pad(parts["kiwi"], ((0, 0), (0, 128 - (D_IDX + H_IDX))))
    return parts


def _attn_project(h, wts, b, t):
    q = _matmul(h, wts["q"], bf16, scale=SM_SCALE_LOG2).reshape(b, t, -1)
    k = _matmul(h, wts["k"], f32).reshape(b, t, N_KV, HEAD_DIM)
    v = _matmul(h, wts["v"], f32).reshape(b, t, N_KV, HEAD_DIM)
    qi_hm = _matmul_heads(h, wts["qi"], D_IDX, bf16)
    kiwi = _matmul(h, wts["kiwi"], f32).reshape(b, t, 128)
    gate = _matmul(h, wts["gate"], bf16).reshape(b, t, -1)
    ki = kiwi[..., :D_IDX]
    wi = kiwi[..., D_IDX:D_IDX + H_IDX]
    return q, k, v, qi_hm, ki, wi, gate


def _pad_rows(x, rows):
    return jnp.pad(x, ((0, 0), (0, rows - x.shape[1])) + ((0, 0),) * (x.ndim - 2))


def kernel(x_prompt, x_sample, cache_k, cache_v, cache_kidx, state_pool, norm_w, final_norm_w,
           attn_w_in, attn_w_out, rel_bias, pool_w_in, pool_group_w, pool_scale, pool_w_out):
    bp, sp, d = x_prompt.shape
    bs, ts, _ = x_sample.shape
    past = cache_k.shape[2]
    assert ATTN_TQ == ATTN_TK and ATTN_TK >= MAX_DISTANCE and SAMPLE_TK % ATTN_TK == 0
    assert sp % ATTN_TQ == 0 and past % SAMPLE_TK == 0 and past % SELECT_TK == 0
    assert past % CHUNK == 0 and ts <= CHUNK and ts % 16 == 0 and ts >= POOL_HIST
    depth = norm_w.shape[0]

    xp = x_prompt.reshape(bp * sp, d)
    xs = x_sample.reshape(bs * ts, d)
    tab_km = _bias_tables(rel_bias, key_major=True)
    tab_qm = _bias_tables(rel_bias, key_major=False)
    outs = {name: [] for name in ("kp", "vp", "kip", "poolp", "ks", "vs", "kis", "pools")}

    for layer in range(depth):
        hp = _rmsnorm(xp, norm_w[layer], bf16)
        hs = _rmsnorm(xs, norm_w[layer], bf16)
        if layer % 2 == 0:
            a = layer // 2
            wts = _attn_weights(attn_w_in[a])
            w_out = attn_w_out[a].astype(bf16)

            q, k, v, qi_hm, ki, wi, gate = _attn_project(hp, wts, bp, sp)
            nt = sp // ATTN_TK
            k_hm = jnp.transpose(k.astype(bf16), (0, 2, 1, 3)).reshape(bp, N_KV, nt, ATTN_TK, HEAD_DIM)
            vt_hm = jnp.transpose(v.astype(bf16).reshape(bp, nt, ATTN_TK, N_KV, HEAD_DIM), (0, 3, 1, 4, 2))
            vt_hm = jnp.concatenate([vt_hm, jnp.ones((bp, N_KV, nt, ONES_ROWS, ATTN_TK), bf16)], axis=3)
            og = _attn_prompt(tab_km, qi_hm, jnp.transpose(wi, (0, 2, 1)), q, gate,
                              ki.astype(bf16).reshape(bp, nt, ATTN_TK, D_IDX), k_hm, vt_hm)
            xp = _matmul(og.reshape(bp * sp, -1), w_out, f32, res=xp)
            outs["kp"].append(k); outs["vp"].append(v); outs["kip"].append(ki)

            q, k, v, qi_hm, ki, wi, gate = _attn_project(hs, wts, bs, ts)
            ckit = jnp.transpose(cache_kidx[a].astype(bf16), (0, 2, 1))
            nkit = jnp.transpose(_pad_rows(ki.astype(bf16), ATTN_TK), (0, 2, 1))
            nk = _pad_rows(k.astype(bf16).reshape(bs, ts, -1), ATTN_TK)
            nv = _pad_rows(v.astype(bf16).reshape(bs, ts, -1), ATTN_TK)
            mb = _select_sample(qi_hm, wi, ckit, nkit, ts)
            og = _attn_sample(tab_qm[:, :, :ts], mb, q, gate, cache_k[a].reshape(bs, past * N_KV, HEAD_DIM),
                              cache_v[a].reshape(bs, past * N_KV, HEAD_DIM), nk, nv)
            xs = _matmul(og.reshape(bs * ts, -1), w_out, f32, res=xs)
            outs["ks"].append(k); outs["vs"].append(v); outs["kis"].append(ki)
        else:
            p = layer // 2
            e = pool_w_in.shape[2] // 2
            w_u = pool_w_in[p][:, :e].astype(bf16)
            w_g = pool_w_in[p][:, e:].astype(bf16)
            gw = pool_group_w[p].astype(bf16)
            scale = pool_scale[p].reshape(1, e)
            w_out = pool_w_out[p].astype(bf16)

            u = _matmul(hp, w_u, f32).reshape(bp, sp, e)
            gate = _matmul(hp, w_g, bf16).reshape(bp, sp, e)
            zg = _pool_mix(u, jnp.zeros((bp, HALO, e), f32), gate, gw, scale, 0)
            xp = _matmul(zg.reshape(bp * sp, e), w_out, f32, res=xp)
            outs["poolp"].append(u[:, sp - POOL_HIST:])

            u = _matmul(hs, w_u, f32).reshape(bs, ts, e)
            gate = _matmul(hs, w_g, bf16).reshape(bs, ts, e)
            hist16 = jnp.pad(state_pool[p], ((0, 0), (HALO - POOL_HIST, 0), (0, 0)))
            zg = _pool_mix(u, hist16, gate, gw, scale, past)
            xs = _matmul(zg.reshape(bs * ts, e), w_out, f32, res=xs)
            outs["pools"].append(u[:, ts - POOL_HIST:])

    y_prompt = _rmsnorm(xp, final_norm_w, f32).reshape(bp, sp, d)
    y_sample = _rmsnorm(xs, final_norm_w, f32).reshape(bs, ts, d)
    return (y_prompt, y_sample, jnp.stack(outs["kp"]), jnp.stack(outs["vp"]), jnp.stack(outs["kip"]),
            jnp.stack(outs["poolp"]), jnp.stack(outs["ks"]), jnp.stack(outs["vs"]), jnp.stack(outs["kis"]),
            jnp.stack(outs["pools"]))
```

```python
import functools
import math

import jax
import jax.numpy as jnp
from jax import lax
from jax.experimental import pallas as pl
from jax.experimental.pallas import tpu as pltpu

CHUNK = 64
N_HEADS = 32
HEAD_DIM = 128
N_KV = 8
GROUP = N_HEADS // N_KV
H_IDX = 32
D_IDX = 64
TOPK_MAX = 256
NUM_BUCKETS = 32
MAX_DISTANCE = 128
FAR_BUCKET = NUM_BUCKETS // 2 - 1
POOL_WINDOWS = (2, 4, 8, 16)
N_POOL_GROUPS = len(POOL_WINDOWS)
POOL_HIST = max(POOL_WINDOWS) - 1
EPS = 1e-6

ATTN_TQ = 256
ATTN_TK = 256
SAMPLE_TK = 512
SELECT_TK = 1024
ONES_ROWS = 16
HALO = 16
VMEM_LIMIT = 52 * 1024 * 1024

f32 = jnp.float32
bf16 = jnp.bfloat16
INT_MIN = -(2 ** 31)
NEG = -1e30
LOG2E = math.log2(math.e)
SM_SCALE_LOG2 = HEAD_DIM ** -0.5 * LOG2E


def _cparams(sem):
    return pltpu.CompilerParams(dimension_semantics=sem, vmem_limit_bytes=VMEM_LIMIT)


def _rmsnorm_kernel(x_ref, w_ref, o_ref):
    x = x_ref[...]
    y = x * lax.rsqrt(jnp.mean(x * x, axis=-1, keepdims=True) + EPS)
    o_ref[...] = (y * w_ref[...]).astype(o_ref.dtype)


def _rmsnorm(x, w, out_dtype, tm=256):
    m, d = x.shape
    tm = min(tm, m)
    assert m % tm == 0
    return pl.pallas_call(
        _rmsnorm_kernel,
        grid=(m // tm,),
        in_specs=[pl.BlockSpec((tm, d), lambda i: (i, 0)), pl.BlockSpec((1, d), lambda i: (0, 0))],
        out_specs=pl.BlockSpec((tm, d), lambda i: (i, 0)),
        out_shape=jax.ShapeDtypeStruct((m, d), out_dtype),
        compiler_params=_cparams(("parallel",)),
        name="rmsnorm",
    )(x, w.reshape(1, d))


def _mm_kernel(a_ref, w_ref, o_ref, *, scale):
    acc = jnp.dot(a_ref[...], w_ref[...], preferred_element_type=f32)
    if scale is not None:
        acc = acc * scale
    o_ref[...] = acc.astype(o_ref.dtype)


def _mm_res_kernel(a_ref, w_ref, r_ref, o_ref):
    acc = jnp.dot(a_ref[...], w_ref[...], preferred_element_type=f32)
    o_ref[...] = (r_ref[...] + acc).astype(o_ref.dtype)


def _matmul(a, w, out_dtype, res=None, scale=None, tm=1024, tn=512):
    m, k = a.shape
    n = w.shape[1]
    tm = min(tm, m)
    tn = min(tn, n)
    assert m % tm == 0 and n % tn == 0
    in_specs = [pl.BlockSpec((tm, k), lambda i, j: (i, 0)), pl.BlockSpec((k, tn), lambda i, j: (0, j))]
    args = [a, w]
    kern = functools.partial(_mm_kernel, scale=scale)
    if res is not None:
        assert scale is None
        in_specs.append(pl.BlockSpec((tm, tn), lambda i, j: (i, j)))
        args.append(res)
        kern = _mm_res_kernel
    return pl.pallas_call(
        kern,
        grid=(m // tm, n // tn),
        in_specs=in_specs,
        out_specs=pl.BlockSpec((tm, tn), lambda i, j: (i, j)),
        out_shape=jax.ShapeDtypeStruct((m, n), out_dtype),
        compiler_params=_cparams(("parallel", "parallel")),
        name="matmul_res" if res is not None else "matmul",
    )(*args)


def _mm_heads_kernel(a_ref, w_ref, o_ref):
    acc = jnp.dot(a_ref[...], w_ref[...], preferred_element_type=f32)
    dh = o_ref.shape[2]
    for hh in range(o_ref.shape[0]):
        o_ref[hh] = acc[:, hh * dh:(hh + 1) * dh].astype(o_ref.dtype)


def _matmul_heads(a, w, dh, out_dtype, tm=1024, tn=512):
    m, k = a.shape
    n = w.shape[1]
    tm = min(tm, m)
    assert m % tm == 0 and n % tn == 0 and tn % dh == 0
    return pl.pallas_call(
        _mm_heads_kernel,
        grid=(m // tm, n // tn),
        in_specs=[pl.BlockSpec((tm, k), lambda i, j: (i, 0)), pl.BlockSpec((k, tn), lambda i, j: (0, j))],
        out_specs=pl.BlockSpec((tn // dh, tm, dh), lambda i, j: (j, i, 0)),
        out_shape=jax.ShapeDtypeStruct((n // dh, m, dh), out_dtype),
        compiler_params=_cparams(("parallel", "parallel")),
        name="matmul_heads",
    )(a, w)


def _rel_bucket(rel):
    nb = NUM_BUCKETS // 2
    ret = jnp.where(rel > 0, nb, 0)
    n = jnp.abs(rel)
    max_exact = nb // 2
    nf = jnp.maximum(n, 1).astype(f32)
    large = max_exact + (jnp.log(nf / max_exact) / math.log(MAX_DISTANCE / max_exact)
                         * (nb - max_exact)).astype(jnp.int32)
    large = jnp.minimum(large, nb - 1)
    return ret + jnp.where(n < max_exact, n, large)


def _bias_table_kernel(rb_ref, bk_ref, o_ref):
    bk = bk_ref[0]

    def head(h, carry):
        acc = jnp.zeros(bk.shape, f32)
        for b in range(NUM_BUCKETS):
            acc = jnp.where(bk == b, rb_ref[b, h], acc)
        o_ref[0, h] = (acc - rb_ref[FAR_BUCKET, h]) * LOG2E
        return carry

    lax.fori_loop(0, N_HEADS, head, 0)


def _bias_tables(rel_bias, key_major, n_queries):
    shape = (ATTN_TK, n_queries) if key_major else (n_queries, ATTN_TK)
    ax_q, ax_k = (1, 0) if key_major else (0, 1)
    qo = lax.broadcasted_iota(jnp.int32, shape, ax_q)
    ko = lax.broadcasted_iota(jnp.int32, shape, ax_k)
    buckets = jnp.stack([_rel_bucket(ko - qo), _rel_bucket(ko - ATTN_TK - qo)])
    return pl.pallas_call(
        _bias_table_kernel,
        grid=(2,),
        in_specs=[pl.BlockSpec(memory_space=pltpu.SMEM),
                  pl.BlockSpec((1,) + shape, lambda a: (a, 0, 0))],
        out_specs=pl.BlockSpec((1, N_HEADS) + shape, lambda a: (a, 0, 0, 0)),
        out_shape=jax.ShapeDtypeStruct((2, N_HEADS) + shape, f32),
        compiler_params=_cparams(("parallel",)),
        name="bias_tables",
    )(rel_bias, buckets)


def _float_order_key(x):
    bits = pltpu.bitcast(x + 0.0, jnp.int32)
    return bits ^ ((bits >> 31) & jnp.int32(0x7FFFFFFF))


def _score_keys_km(qi_ref, w_t, ki_t):
    tq = qi_ref.shape[1]
    heads_per_dot = 8
    acc = jnp.zeros((ki_t.shape[0], tq), f32)
    for hc in range(H_IDX // heads_per_dot):
        rhs = qi_ref[hc * heads_per_dot:(hc + 1) * heads_per_dot].reshape(heads_per_dot * tq, D_IDX)
        d = lax.dot_general(ki_t, rhs, (((1,), (1,)), ((), ())), preferred_element_type=f32)
        for hh in range(heads_per_dot):
            h = hc * heads_per_dot + hh
            acc = acc + jnp.maximum(d[:, hh * tq:(hh + 1) * tq], 0.0) * w_t[h:h + 1, :]
    return _float_order_key(acc)


def _kth_largest_key_km(key_scr, n_tiles, k_sel):
    tk, tq = key_scr.shape[1], key_scr.shape[2]
    empty_tile = key_scr.shape[0] - 1

    def bit_step(bi, tvec):
        cand = tvec + lax.shift_left(jnp.int32(1), 31 - bi)

        def count(t, c):
            one = jnp.where(key_scr[t] >= cand, 1.0, 0.0)
            return c + jnp.sum(one.reshape(tk // 8, 8, tq), axis=0)

        def count_pair(tau, c):
            t1 = 2 * tau + 1
            return count(jnp.where(t1 < n_tiles, t1, empty_tile), count(2 * tau, c))

        c = lax.fori_loop(0, (n_tiles + 1) // 2, count_pair, jnp.zeros((8, tq), f32))
        cnt = jnp.sum(c, axis=0, keepdims=True)
        return jnp.where(cnt >= float(k_sel), cand, tvec)

    return lax.fori_loop(0, 32, bit_step, jnp.full((1, tq), INT_MIN, jnp.int32))


def _softmax_stage(s_ref, p_ref, mask_bias, pv_prev, m_scr, acc_scr):
    lg = s_ref[...] + mask_bias
    m_prev = m_scr[...]
    m_new = jnp.maximum(m_prev, jnp.max(lg, axis=0, keepdims=True))
    alpha = jnp.exp2(m_prev - m_new)
    p_ref[...] = jnp.exp2(lg - m_new).astype(bf16)
    acc_scr[...] = (acc_scr[...] + pv_prev) * alpha
    m_scr[...] = m_new


def _attn_prompt_kernel(qi_ref, wi_ref, q_ref, gate_ref, ki_ref, k_ref, vt_ref, tab_ref, o_ref,
                        key_scr, mb_scr, sa_scr, sb_scr, pa_scr, pb_scr, m_scr, acc_scr, *, k_sel):
    i = pl.program_id(1)
    g = pl.program_id(2)
    tq, tk = ATTN_TQ, ATTN_TK
    masked_tile = mb_scr.shape[0] - 1
    ko = lax.broadcasted_iota(jnp.int32, (tk, tq), 0)
    qo = lax.broadcasted_iota(jnp.int32, (tk, tq), 1)
    causal = (ko // CHUNK) <= (qo // CHUNK)

    @pl.when(g == 0)
    def _select():
        w_t = wi_ref[0] * (D_IDX ** -0.5 * H_IDX ** -0.5)

        def score_tile(t, carry):
            key_scr[t] = _score_keys_km(qi_ref, w_t, ki_ref[0, t])
            return carry

        lax.fori_loop(0, i + 1, score_tile, 0)
        key_scr[i] = jnp.where(causal, key_scr[i], INT_MIN)
        key_scr[masked_tile] = jnp.full((tk, tq), INT_MIN, jnp.int32)
        mb_scr[masked_tile] = jnp.full((tk, tq), NEG, f32)
        tvec = _kth_largest_key_km(key_scr, i + 1, k_sel)

        def mask_tile(t, carry):
            mb_scr[t] = jnp.where(key_scr[t] >= tvec, 0.0, NEG)
            return carry

        lax.fori_loop(0, i + 1, mask_tile, 0)
        mb_scr[i] = jnp.where(causal, mb_scr[i], NEG)

    n_far = jnp.maximum(i - 1, 0)
    n_pairs = (n_far + 1) // 2
    t_prev = jnp.maximum(i - 1, 0)
    mask_prev = jnp.where(i >= 1, i - 1, masked_tile)
    m_scr[...] = jnp.full(m_scr.shape, NEG, f32)
    acc_scr[...] = jnp.zeros(acc_scr.shape, f32)
    pb_scr[...] = jnp.zeros(pb_scr.shape, bf16)
    qs = jnp.concatenate([q_ref[0, :, hh * HEAD_DIM:(hh + 1) * HEAD_DIM] for hh in range(GROUP)], axis=0)

    def qk(t):
        return lax.dot_general(k_ref[0, 0, t], qs, (((1,), (1,)), ((), ())), preferred_element_type=f32)

    def pv(t, p_ref):
        return jnp.dot(vt_ref[0, 0, t], p_ref[...], preferred_element_type=f32)

    def far_mask(t):
        return jnp.concatenate([mb_scr[jnp.where(t < n_far, t, masked_tile)]] * GROUP, axis=1)

    def near_mask(mask_t, a):
        mbt = mb_scr[mask_t]
        return jnp.concatenate([mbt + tab_ref[a, hh] for hh in range(GROUP)], axis=1)

    sa_scr[...] = qk(0)

    def pair_step(tau, carry):
        t0 = 2 * tau
        pv_b = pv(jnp.maximum(t0 - 1, 0), pb_scr)
        sb_scr[...] = qk(jnp.minimum(t0 + 1, n_far - 1))
        _softmax_stage(sa_scr, pa_scr, far_mask(t0), pv_b, m_scr, acc_scr)
        pv_a = pv(t0, pa_scr)
        sa_scr[...] = qk(jnp.minimum(t0 + 2, n_far))
        _softmax_stage(sb_scr, pb_scr, far_mask(t0 + 1), pv_a, m_scr, acc_scr)
        return carry

    lax.fori_loop(0, n_pairs, pair_step, 0)
    pv_b = pv(jnp.clip(2 * n_pairs - 1, 0, jnp.maximum(n_far - 1, 0)), pb_scr)
    sb_scr[...] = qk(i)
    _softmax_stage(sa_scr, pa_scr, near_mask(mask_prev, 1), pv_b, m_scr, acc_scr)
    pv_a = pv(t_prev, pa_scr)
    _softmax_stage(sb_scr, pb_scr, near_mask(i, 0), pv_a, m_scr, acc_scr)
    acc = acc_scr[...] + pv(i, pb_scr)
    o_t = acc[0:HEAD_DIM] / acc[HEAD_DIM:HEAD_DIM + 1]
    for hh in range(GROUP):
        o = o_t[:, hh * tq:(hh + 1) * tq].T
        gt = gate_ref[0, :, hh * HEAD_DIM:(hh + 1) * HEAD_DIM].astype(f32)
        o_ref[0, :, hh * HEAD_DIM:(hh + 1) * HEAD_DIM] = (o * (gt * jax.nn.sigmoid(gt))).astype(o_ref.dtype)


def _attn_prompt(tab, qi_hm, wi_t, q, gate, ki, k_hm, vt_hm):
    b, s, width = q.shape
    tq, tk = ATTN_TQ, ATTN_TK
    nt = s // tk
    nq = s // tq
    k_sel = min(TOPK_MAX, s // 4)
    gw = GROUP * HEAD_DIM
    vrows = HEAD_DIM + ONES_ROWS
    return pl.pallas_call(
        functools.partial(_attn_prompt_kernel, k_sel=k_sel),
        grid=(b, nq, N_KV),
        in_specs=[
            pl.BlockSpec((H_IDX, tq, D_IDX), lambda bb, i, g: (0, bb * nq + i, 0)),
            pl.BlockSpec((1, H_IDX, tq), lambda bb, i, g: (bb, 0, i)),
            pl.BlockSpec((1, tq, gw), lambda bb, i, g: (bb, i, g)),
            pl.BlockSpec((1, tq, gw), lambda bb, i, g: (bb, i, g)),
            pl.BlockSpec((1, nt, tk, D_IDX), lambda bb, i, g: (bb, 0, 0, 0)),
            pl.BlockSpec((1, 1, nt, tk, HEAD_DIM), lambda bb, i, g: (bb, g, 0, 0, 0)),
            pl.BlockSpec((1, 1, nt, vrows, tk), lambda bb, i, g: (bb, g, 0, 0, 0)),
            pl.BlockSpec((2, GROUP, tk, tq), lambda bb, i, g: (0, g, 0, 0)),
        ],
        out_specs=pl.BlockSpec((1, tq, gw), lambda bb, i, g: (bb, i, g)),
        out_shape=jax.ShapeDtypeStruct((b, s, width), bf16),
        scratch_shapes=[
            pltpu.VMEM((nt + 1, tk, tq), jnp.int32),
            pltpu.VMEM((nt + 1, tk, tq), f32),
            pltpu.VMEM((tk, GROUP * tq), f32),
            pltpu.VMEM((tk, GROUP * tq), f32),
            pltpu.VMEM((tk, GROUP * tq), bf16),
            pltpu.VMEM((tk, GROUP * tq), bf16),
            pltpu.VMEM((1, GROUP * tq), f32),
            pltpu.VMEM((vrows, GROUP * tq), f32),
        ],
        compiler_params=_cparams(("parallel", "arbitrary", "arbitrary")),
        name="attn_prompt",
    )(qi_hm, wi_t, q, gate, ki, k_hm, vt_hm, tab)


def _score_keys_qm(qi_ref, w, kt):
    tq = qi_ref.shape[1]
    heads_per_dot = 8
    acc = jnp.zeros((tq, kt.shape[1]), f32)
    for hc in range(H_IDX // heads_per_dot):
        lhs = qi_ref[hc * heads_per_dot:(hc + 1) * heads_per_dot].reshape(heads_per_dot * tq, D_IDX)
        d = jnp.dot(lhs, kt, preferred_element_type=f32)
        for hh in range(heads_per_dot):
            h = hc * heads_per_dot + hh
            acc = acc + jnp.maximum(d[hh * tq:(hh + 1) * tq], 0.0) * w[:, h:h + 1]
    return _float_order_key(acc)


def _kth_largest_key_qm(key_scr, k_sel):
    tq, width = key_scr.shape

    def bit_step(bi, tvec):
        cand = tvec + lax.shift_left(jnp.int32(1), 31 - bi)
        c = jnp.zeros((tq, 128), f32)
        for s in range(width // 128):
            c = c + jnp.where(key_scr[:, s * 128:(s + 1) * 128] >= cand, 1.0, 0.0)
        cnt = jnp.sum(c, axis=1, keepdims=True)
        return jnp.where(cnt >= float(k_sel), cand, tvec)

    return lax.fori_loop(0, 32, bit_step, jnp.full((tq, 1), INT_MIN, jnp.int32))


def _select_sample_kernel(qi_ref, wi_ref, ckit_ref, nkit_ref, mb_ref, key_scr, *, k_sel, n_new):
    tq = qi_ref.shape[1]
    tk = ATTN_TK
    p = ckit_ref.shape[2]
    real = lax.broadcasted_iota(jnp.int32, (tq, tk), 1) < n_new
    w = wi_ref[0] * (D_IDX ** -0.5 * H_IDX ** -0.5)
    for c in range(p // SELECT_TK):
        cols = slice(c * SELECT_TK, (c + 1) * SELECT_TK)
        key_scr[:, cols] = _score_keys_qm(qi_ref, w, ckit_ref[0, :, cols])
    key_scr[:, p:p + tk] = jnp.where(real, _score_keys_qm(qi_ref, w, nkit_ref[0]), INT_MIN)
    tvec = _kth_largest_key_qm(key_scr, k_sel)
    mb_ref[0, :, 0:p] = jnp.where(key_scr[:, 0:p] >= tvec, 0.0, NEG)
    mb_ref[0, :, p:p + tk] = jnp.where(real & (key_scr[:, p:p + tk] >= tvec), 0.0, NEG)


def _select_sample(qi_hm, wi, ckit, nkit, n_new):
    b, t, _ = wi.shape
    p = ckit.shape[2]
    tk = ATTN_TK
    k_sel = min(TOPK_MAX, (p + n_new) // 4)
    return pl.pallas_call(
        functools.partial(_select_sample_kernel, k_sel=k_sel, n_new=n_new),
        grid=(b,),
        in_specs=[
            pl.BlockSpec((H_IDX, t, D_IDX), lambda bb: (0, bb, 0)),
            pl.BlockSpec((1, t, H_IDX), lambda bb: (bb, 0, 0)),
            pl.BlockSpec((1, D_IDX, p), lambda bb: (bb, 0, 0)),
            pl.BlockSpec((1, D_IDX, tk), lambda bb: (bb, 0, 0)),
        ],
        out_specs=pl.BlockSpec((1, t, p + tk), lambda bb: (bb, 0, 0)),
        out_shape=jax.ShapeDtypeStruct((b, t, p + tk), f32),
        scratch_shapes=[pltpu.VMEM((t, p + tk), jnp.int32)],
        compiler_params=_cparams(("parallel",)),
        name="select_sample",
    )(qi_hm, wi, ckit, nkit)


def _attend_tile_qm(q_ref, k_of, v_of, mbt, bias_of, s_scr, p_scr, m_scr, l_scr, acc_scr):
    w = mbt.shape[1]
    mb4 = jnp.concatenate([mbt] * GROUP, axis=0)
    for g in range(N_KV):
        qs = jnp.concatenate(
            [q_ref[0, :, (g * GROUP + hh) * HEAD_DIM:(g * GROUP + hh + 1) * HEAD_DIM] for hh in range(GROUP)], axis=0)
        s_scr[g, :, 0:w] = lax.dot_general(qs, k_of(g), (((1,), (1,)), ((), ())), preferred_element_type=f32)
    for g in range(N_KV):
        lg = s_scr[g, :, 0:w] + mb4
        if bias_of is not None:
            lg = lg + bias_of(g)
        m_prev = m_scr[g]
        m_new = jnp.maximum(m_prev, jnp.max(lg, axis=1, keepdims=True))
        alpha = jnp.exp2(m_prev - m_new)
        p = jnp.exp2(lg - m_new)
        l_scr[g] = alpha * l_scr[g] + jnp.sum(p, axis=1, keepdims=True)
        acc_scr[g] = alpha * acc_scr[g]
        m_scr[g] = m_new
        p_scr[g, :, 0:w] = p.astype(bf16)
    for g in range(N_KV):
        acc_scr[g] = acc_scr[g] + jnp.dot(p_scr[g, :, 0:w], v_of(g), preferred_element_type=f32)


def _attn_sample_kernel(q_ref, gate_ref, ck_ref, cv_ref, nk_ref, nv_ref, mbc_ref, mbn_ref, tab_ref, o_ref,
                        s_scr, p_scr, m_scr, l_scr, acc_scr):
    j = pl.program_id(1)
    nj = pl.num_programs(1)
    t = q_ref.shape[1]
    tk = ATTN_TK
    n_keys = mbc_ref.shape[2]

    @pl.when(j == 0)
    def _init():
        m_scr[...] = jnp.full(m_scr.shape, NEG, f32)
        l_scr[...] = jnp.zeros(l_scr.shape, f32)
        acc_scr[...] = jnp.zeros(acc_scr.shape, f32)

    def head_cols(g):
        return slice(g * HEAD_DIM, (g + 1) * HEAD_DIM)

    def cache_head(ref, g):
        return ref[0, pl.ds(g, n_keys, stride=N_KV), :].astype(bf16)

    def cache_tile(last):
        bias_of = None
        if last:
            far = jnp.zeros((t, n_keys - tk), f32)

            def bias_of(g):
                return jnp.concatenate(
                    [jnp.concatenate([far, tab_ref[1, g * GROUP + hh]], axis=1) for hh in range(GROUP)], axis=0)

        _attend_tile_qm(q_ref, functools.partial(cache_head, ck_ref), functools.partial(cache_head, cv_ref),
                        mbc_ref[0], bias_of, s_scr, p_scr, m_scr, l_scr, acc_scr)

    @pl.when(j < nj - 1)
    def _far():
        cache_tile(False)

    @pl.when(j == nj - 1)
    def _near():
        cache_tile(True)
        _attend_tile_qm(q_ref, lambda g: nk_ref[0, :, head_cols(g)], lambda g: nv_ref[0, :, head_cols(g)],
                        mbn_ref[0],
                        lambda g: jnp.concatenate([tab_ref[0, g * GROUP + hh] for hh in range(GROUP)], axis=0),
                        s_scr, p_scr, m_scr, l_scr, acc_scr)
        for g in range(N_KV):
            o = acc_scr[g] / l_scr[g]
            for hh in range(GROUP):
                hcols = head_cols(g * GROUP + hh)
                gt = gate_ref[0, :, hcols].astype(f32)
                o_ref[0, :, hcols] = (o[hh * t:(hh + 1) * t] * (gt * jax.nn.sigmoid(gt))).astype(o_ref.dtype)


def _attn_sample(tab, mb, q, gate, ck, cv, nk, nv):
    b, t, width = q.shape
    tk = ATTN_TK
    p = ck.shape[1] // N_KV
    nj = p // SAMPLE_TK
    kvw = N_KV * HEAD_DIM
    return pl.pallas_call(
        _attn_sample_kernel,
        grid=(b, nj),
        in_specs=[
            pl.BlockSpec((1, t, width), lambda bb, j: (bb, 0, 0)),
            pl.BlockSpec((1, t, width), lambda bb, j: (bb, 0, 0)),
            pl.BlockSpec((1, SAMPLE_TK * N_KV, HEAD_DIM), lambda bb, j: (bb, j, 0)),
            pl.BlockSpec((1, SAMPLE_TK * N_KV, HEAD_DIM), lambda bb, j: (bb, j, 0)),
            pl.BlockSpec((1, tk, kvw), lambda bb, j: (bb, 0, 0)),
            pl.BlockSpec((1, tk, kvw), lambda bb, j: (bb, 0, 0)),
            pl.BlockSpec((1, t, SAMPLE_TK), lambda bb, j: (bb, 0, j)),
            pl.BlockSpec((1, t, tk), lambda bb, j: (bb, 0, p // tk)),
            pl.BlockSpec((2, N_HEADS, t, tk), lambda bb, j: (0, 0, 0, 0)),
        ],
        out_specs=pl.BlockSpec((1, t, width), lambda bb, j: (bb, 0, 0)),
        out_shape=jax.ShapeDtypeStruct((b, t, width), bf16),
        scratch_shapes=[
            pltpu.VMEM((N_KV, GROUP * t, SAMPLE_TK), f32),
            pltpu.VMEM((N_KV, GROUP * t, SAMPLE_TK), bf16),
            pltpu.VMEM((N_KV, GROUP * t, 1), f32),
            pltpu.VMEM((N_KV, GROUP * t, 1), f32),
            pltpu.VMEM((N_KV, GROUP * t, HEAD_DIM), f32),
        ],
        compiler_params=_cparams(("parallel", "arbitrary")),
        name="attn_sample",
    )(q, gate, ck, cv, nk, nv, mb, mb, tab)


def _pool_kernel(u_ref, hist_ref, gate_ref, gw_ref, scale_ref, o_ref, ext_scr, *, start):
    i = pl.program_id(1)
    tm = u_ref.shape[1]
    gwid = gw_ref.shape[1]

    @pl.when(i == 0)
    def _first():
        ext_scr[0:HALO, :] = hist_ref[0]

    @pl.when(i > 0)
    def _carry():
        ext_scr[0:HALO, :] = ext_scr[tm:tm + HALO, :]

    ext_scr[HALO:HALO + tm, :] = u_ref[0]
    pos = start + i * tm + lax.broadcasted_iota(jnp.int32, (tm, 1), 0)
    for gi, win in enumerate(POOL_WINDOWS):
        lo, hi = gi * gwid, (gi + 1) * gwid
        s = ext_scr[:, lo:hi]
        shift = 1
        while shift < win:
            s = s + pltpu.roll(s, shift, 0)
            shift *= 2
        s = s[HALO:HALO + tm]
        tok = ext_scr[HALO:HALO + tm, lo:hi]
        cnt = jnp.minimum(pos + 1, win).astype(f32)
        mix = (s / cnt - tok).astype(bf16)
        z = jnp.dot(mix, gw_ref[gi], preferred_element_type=f32) * scale_ref[:, lo:hi]
        gt = gate_ref[0, :, lo:hi].astype(f32)
        o_ref[0, :, lo:hi] = (z * (gt * jax.nn.sigmoid(gt))).astype(o_ref.dtype)


def _pool_mix(u, hist16, gate, group_w, scale, start, tm=256):
    b, t, e = u.shape
    tm = min(tm, t)
    gwid = e // N_POOL_GROUPS
    return pl.pallas_call(
        functools.partial(_pool_kernel, start=start),
        grid=(b, t // tm),
        in_specs=[
            pl.BlockSpec((1, tm, e), lambda bb, i: (bb, i, 0)),
            pl.BlockSpec((1, HALO, e), lambda bb, i: (bb, 0, 0)),
            pl.BlockSpec((1, tm, e), lambda bb, i: (bb, i, 0)),
            pl.BlockSpec((N_POOL_GROUPS, gwid, gwid), lambda bb, i: (0, 0, 0)),
            pl.BlockSpec((1, e), lambda bb, i: (0, 0)),
        ],
        out_specs=pl.BlockSpec((1, tm, e), lambda bb, i: (bb, i, 0)),
        out_shape=jax.ShapeDtypeStruct((b, t, e), bf16),
        scratch_shapes=[pltpu.VMEM((HALO + tm, e), f32)],
        compiler_params=_cparams(("parallel", "arbitrary")),
        name="pool_mix",
    )(u, hist16, gate, group_w, scale)


def _attn_weights(w_in):
    qw, kw_ = N_HEADS * HEAD_DIM, N_KV * HEAD_DIM
    o = 0
    parts = {}
    for name, wdt in (("q", qw), ("k", kw_), ("v", kw_), ("qi", H_IDX * D_IDX), ("kiwi", D_IDX + H_IDX),
                      ("gate", qw)):
        parts[name] = w_in[:, o:o + wdt].astype(bf16)
        o += wdt
    parts["kiwi"] = jnp.pad(parts["kiwi"], ((0, 0), (0, 128 - (D_IDX + H_IDX))))
    return parts


def _attn_project(h, wts, b, t):
    q = _matmul(h, wts["q"], bf16, scale=SM_SCALE_LOG2).reshape(b, t, -1)
    k = _matmul(h, wts["k"], f32).reshape(b, t, N_KV, HEAD_DIM)
    v = _matmul(h, wts["v"], f32).reshape(b, t, N_KV, HEAD_DIM)
    qi_hm = _matmul_heads(h, wts["qi"], D_IDX, bf16)
    kiwi = _matmul(h, wts["kiwi"], f32).reshape(b, t, 128)
    gate = _matmul(h, wts["gate"], bf16).reshape(b, t, -1)
    ki = kiwi[..., :D_IDX]
    wi = kiwi[..., D_IDX:D_IDX + H_IDX]
    return q, k, v, qi_hm, ki, wi, gate


def _pad_rows(x, rows):
    return jnp.pad(x, ((0, 0), (0, rows - x.shape[1])) + ((0, 0),) * (x.ndim - 2))


def kernel(x_prompt, x_sample, cache_k, cache_v, cache_kidx, state_pool, norm_w, final_norm_w,
           attn_w_in, attn_w_out, rel_bias, pool_w_in, pool_group_w, pool_scale, pool_w_out):
    bp, sp, d = x_prompt.shape
    bs, ts, _ = x_sample.shape
    past = cache_k.shape[2]
    assert ATTN_TQ == ATTN_TK and ATTN_TK >= MAX_DISTANCE and SAMPLE_TK % ATTN_TK == 0
    assert sp % ATTN_TQ == 0 and past % SAMPLE_TK == 0 and past % SELECT_TK == 0
    assert past % CHUNK == 0 and ts <= CHUNK and ts % 16 == 0 and ts >= POOL_HIST
    depth = norm_w.shape[0]

    xp = x_prompt.reshape(bp * sp, d)
    xs = x_sample.reshape(bs * ts, d)
    tab_km = _bias_tables(rel_bias, True, ATTN_TQ)
    tab_qm = _bias_tables(rel_bias, False, ts)
    outs = {name: [] for name in ("kp", "vp", "kip", "poolp", "ks", "vs", "kis", "pools")}

    for layer in range(depth):
        hp = _rmsnorm(xp, norm_w[layer], bf16)
        hs = _rmsnorm(xs, norm_w[layer], bf16)
        if layer % 2 == 0:
            a = layer // 2
            wts = _attn_weights(attn_w_in[a])
            w_out = attn_w_out[a].astype(bf16)

            q, k, v, qi_hm, ki, wi, gate = _attn_project(hp, wts, bp, sp)
            nt = sp // ATTN_TK
            k_hm = jnp.transpose(k.astype(bf16), (0, 2, 1, 3)).reshape(bp, N_KV, nt, ATTN_TK, HEAD_DIM)
            vt_hm = jnp.transpose(v.astype(bf16).reshape(bp, nt, ATTN_TK, N_KV, HEAD_DIM), (0, 3, 1, 4, 2))
            vt_hm = jnp.concatenate([vt_hm, jnp.ones((bp, N_KV, nt, ONES_ROWS, ATTN_TK), bf16)], axis=3)
            og = _attn_prompt(tab_km, qi_hm, jnp.transpose(wi, (0, 2, 1)), q, gate,
                              ki.astype(bf16).reshape(bp, nt, ATTN_TK, D_IDX), k_hm, vt_hm)
            xp = _matmul(og.reshape(bp * sp, -1), w_out, f32, res=xp)
            outs["kp"].append(k); outs["vp"].append(v); outs["kip"].append(ki)

            q, k, v, qi_hm, ki, wi, gate = _attn_project(hs, wts, bs, ts)
            ckit = jnp.transpose(cache_kidx[a].astype(bf16), (0, 2, 1))
            nkit = jnp.transpose(_pad_rows(ki.astype(bf16), ATTN_TK), (0, 2, 1))
            nk = _pad_rows(k.astype(bf16).reshape(bs, ts, -1), ATTN_TK)
            nv = _pad_rows(v.astype(bf16).reshape(bs, ts, -1), ATTN_TK)
            mb = _select_sample(qi_hm, wi, ckit, nkit, ts)
            og = _attn_sample(tab_qm, mb, q, gate, cache_k[a].reshape(bs, past * N_KV, HEAD_DIM),
                              cache_v[a].reshape(bs, past * N_KV, HEAD_DIM), nk, nv)
            xs = _matmul(og.reshape(bs * ts, -1), w_out, f32, res=xs)
            outs["ks"].append(k); outs["vs"].append(v); outs["kis"].append(ki)
        else:
            p = layer // 2
            e = pool_w_in.shape[2] // 2
            w_u = pool_w_in[p][:, :e].astype(bf16)
            w_g = pool_w_in[p][:, e:].astype(bf16)
            gw = pool_group_w[p].astype(bf16)
            scale = pool_scale[p].reshape(1, e)
            w_out = pool_w_out[p].astype(bf16)

            u = _matmul(hp, w_u, f32).reshape(bp, sp, e)
            gate = _matmul(hp, w_g, bf16).reshape(bp, sp, e)
            zg = _pool_mix(u, jnp.zeros((bp, HALO, e), f32), gate, gw, scale, 0)
            xp = _matmul(zg.reshape(bp * sp, e), w_out, f32, res=xp)
            outs["poolp"].append(u[:, sp - POOL_HIST:])

            u = _matmul(hs, w_u, f32).reshape(bs, ts, e)
            gate = _matmul(hs, w_g, bf16).reshape(bs, ts, e)
            hist16 = jnp.pad(state_pool[p], ((0, 0), (HALO - POOL_HIST, 0), (0, 0)))
            zg = _pool_mix(u, hist16, gate, gw, scale, past)
            xs = _matmul(zg.reshape(bs * ts, e), w_out, f32, res=xs)
            outs["pools"].append(u[:, ts - POOL_HIST:])

    y_prompt = _rmsnorm(xp, final_norm_w, f32).reshape(bp, sp, d)
    y_sample = _rmsnorm(xs, final_norm_w, f32).reshape(bs, ts, d)
    return (y_prompt, y_sample, jnp.stack(outs["kp"]), jnp.stack(outs["vp"]), jnp.stack(outs["kip"]),
            jnp.stack(outs["poolp"]), jnp.stack(outs["ks"]), jnp.stack(outs["vs"]), jnp.stack(outs["kis"]),
            jnp.stack(outs["pools"]))
```

```python
import functools
import math

import jax
import jax.numpy as jnp
from jax import lax
from jax.experimental import pallas as pl
from jax.experimental.pallas import tpu as pltpu

CHUNK = 64
N_HEADS = 32
HEAD_DIM = 128
N_KV = 8
GROUP = N_HEADS // N_KV
H_IDX = 32
D_IDX = 64
TOPK_MAX = 256
NUM_BUCKETS = 32
MAX_DISTANCE = 128
FAR_BUCKET = NUM_BUCKETS // 2 - 1
POOL_WINDOWS = (2, 4, 8, 16)
N_POOL_GROUPS = len(POOL_WINDOWS)
POOL_HIST = max(POOL_WINDOWS) - 1
EPS = 1e-6

ATTN_TQ = 256
ATTN_TK = 256
SAMPLE_TK = 1024
WIDE_TN = 1024
SELECT_TK = 1024
ONES_ROWS = 16
HALO = 16
VMEM_LIMIT = 52 * 1024 * 1024

f32 = jnp.float32
bf16 = jnp.bfloat16
INT_MIN = -(2 ** 31)
NEG = -1e30
LOG2E = math.log2(math.e)
SM_SCALE_LOG2 = HEAD_DIM ** -0.5 * LOG2E


def _cparams(sem):
    return pltpu.CompilerParams(dimension_semantics=sem, vmem_limit_bytes=VMEM_LIMIT)


def _rmsnorm_kernel(x_ref, w_ref, o_ref):
    x = x_ref[...]
    y = x * lax.rsqrt(jnp.mean(x * x, axis=-1, keepdims=True) + EPS)
    o_ref[...] = (y * w_ref[...]).astype(o_ref.dtype)


def _rmsnorm(x, w, out_dtype, tm=512):
    m, d = x.shape
    tm = min(tm, m)
    assert m % tm == 0
    return pl.pallas_call(
        _rmsnorm_kernel,
        grid=(m // tm,),
        in_specs=[pl.BlockSpec((tm, d), lambda i: (i, 0)), pl.BlockSpec((1, d), lambda i: (0, 0))],
        out_specs=pl.BlockSpec((tm, d), lambda i: (i, 0)),
        out_shape=jax.ShapeDtypeStruct((m, d), out_dtype),
        compiler_params=_cparams(("parallel",)),
        name="rmsnorm",
    )(x, w.reshape(1, d))


def _mm_kernel(a_ref, w_ref, o_ref, *, scale):
    acc = jnp.dot(a_ref[...], w_ref[...], preferred_element_type=f32)
    if scale is not None:
        acc = acc * scale
    o_ref[...] = acc.astype(o_ref.dtype)


def _mm_res_kernel(a_ref, w_ref, r_ref, o_ref):
    acc = jnp.dot(a_ref[...], w_ref[...], preferred_element_type=f32)
    o_ref[...] = (r_ref[...] + acc).astype(o_ref.dtype)


def _matmul(a, w, out_dtype, res=None, scale=None, tm=1024, tn=512):
    m, k = a.shape
    n = w.shape[1]
    tm = min(tm, m)
    tn = min(tn, n)
    assert m % tm == 0 and n % tn == 0
    in_specs = [pl.BlockSpec((tm, k), lambda i, j: (i, 0)), pl.BlockSpec((k, tn), lambda i, j: (0, j))]
    args = [a, w]
    kern = functools.partial(_mm_kernel, scale=scale)
    if res is not None:
        assert scale is None
        in_specs.append(pl.BlockSpec((tm, tn), lambda i, j: (i, j)))
        args.append(res)
        kern = _mm_res_kernel
    return pl.pallas_call(
        kern,
        grid=(m // tm, n // tn),
        in_specs=in_specs,
        out_specs=pl.BlockSpec((tm, tn), lambda i, j: (i, j)),
        out_shape=jax.ShapeDtypeStruct((m, n), out_dtype),
        compiler_params=_cparams(("parallel", "parallel")),
        name="matmul_res" if res is not None else "matmul",
    )(*args)


def _mm_heads_kernel(a_ref, w_ref, o_ref):
    acc = jnp.dot(a_ref[...], w_ref[...], preferred_element_type=f32)
    dh = o_ref.shape[2]
    for hh in range(o_ref.shape[0]):
        o_ref[hh] = acc[:, hh * dh:(hh + 1) * dh].astype(o_ref.dtype)


def _matmul_heads(a, w, dh, out_dtype, tm=1024, tn=512):
    m, k = a.shape
    n = w.shape[1]
    tm = min(tm, m)
    assert m % tm == 0 and n % tn == 0 and tn % dh == 0
    return pl.pallas_call(
        _mm_heads_kernel,
        grid=(m // tm, n // tn),
        in_specs=[pl.BlockSpec((tm, k), lambda i, j: (i, 0)), pl.BlockSpec((k, tn), lambda i, j: (0, j))],
        out_specs=pl.BlockSpec((tn // dh, tm, dh), lambda i, j: (j, i, 0)),
        out_shape=jax.ShapeDtypeStruct((n // dh, m, dh), out_dtype),
        compiler_params=_cparams(("parallel", "parallel")),
        name="matmul_heads",
    )(a, w)


def _rel_bucket(rel):
    nb = NUM_BUCKETS // 2
    ret = jnp.where(rel > 0, nb, 0)
    n = jnp.abs(rel)
    max_exact = nb // 2
    nf = jnp.maximum(n, 1).astype(f32)
    large = max_exact + (jnp.log(nf / max_exact) / math.log(MAX_DISTANCE / max_exact)
                         * (nb - max_exact)).astype(jnp.int32)
    large = jnp.minimum(large, nb - 1)
    return ret + jnp.where(n < max_exact, n, large)


def _bias_table_kernel(rb_ref, bk_ref, o_ref):
    bk = bk_ref[0]

    def head(h, carry):
        acc = jnp.zeros(bk.shape, f32)
        for b in range(NUM_BUCKETS):
            acc = jnp.where(bk == b, rb_ref[b, h], acc)
        o_ref[0, h] = (acc - rb_ref[FAR_BUCKET, h]) * LOG2E
        return carry

    lax.fori_loop(0, N_HEADS, head, 0)


def _bias_tables(rel_bias, key_major, n_queries):
    shape = (ATTN_TK, n_queries) if key_major else (n_queries, ATTN_TK)
    ax_q, ax_k = (1, 0) if key_major else (0, 1)
    qo = lax.broadcasted_iota(jnp.int32, shape, ax_q)
    ko = lax.broadcasted_iota(jnp.int32, shape, ax_k)
    buckets = jnp.stack([_rel_bucket(ko - qo), _rel_bucket(ko - ATTN_TK - qo)])
    return pl.pallas_call(
        _bias_table_kernel,
        grid=(2,),
        in_specs=[pl.BlockSpec(memory_space=pltpu.SMEM),
                  pl.BlockSpec((1,) + shape, lambda a: (a, 0, 0))],
        out_specs=pl.BlockSpec((1, N_HEADS) + shape, lambda a: (a, 0, 0, 0)),
        out_shape=jax.ShapeDtypeStruct((2, N_HEADS) + shape, f32),
        compiler_params=_cparams(("parallel",)),
        name="bias_tables",
    )(rel_bias, buckets)


def _float_order_key(x):
    bits = pltpu.bitcast(x + 0.0, jnp.int32)
    return bits ^ ((bits >> 31) & jnp.int32(0x7FFFFFFF))


def _score_keys_km(qi_ref, w_t, ki_t):
    tq = qi_ref.shape[1]
    heads_per_dot = 8
    acc = jnp.zeros((ki_t.shape[0], tq), f32)
    for hc in range(H_IDX // heads_per_dot):
        rhs = qi_ref[hc * heads_per_dot:(hc + 1) * heads_per_dot].reshape(heads_per_dot * tq, D_IDX)
        d = lax.dot_general(ki_t, rhs, (((1,), (1,)), ((), ())), preferred_element_type=f32)
        for hh in range(heads_per_dot):
            h = hc * heads_per_dot + hh
            acc = acc + jnp.maximum(d[:, hh * tq:(hh + 1) * tq], 0.0) * w_t[h:h + 1, :]
    return _float_order_key(acc)


def _kth_largest_key_km(key_scr, n_tiles, k_sel):
    tk, tq = key_scr.shape[1], key_scr.shape[2]
    empty_tile = key_scr.shape[0] - 1

    def bit_step(bi, tvec):
        cand = tvec + lax.shift_left(jnp.int32(1), 31 - bi)

        def count(t, c):
            one = jnp.where(key_scr[t] >= cand, 1.0, 0.0)
            return c + jnp.sum(one.reshape(tk // 8, 8, tq), axis=0)

        def count_pair(tau, c):
            t1 = 2 * tau + 1
            return count(jnp.where(t1 < n_tiles, t1, empty_tile), count(2 * tau, c))

        c = lax.fori_loop(0, (n_tiles + 1) // 2, count_pair, jnp.zeros((8, tq), f32))
        cnt = jnp.sum(c, axis=0, keepdims=True)
        return jnp.where(cnt >= float(k_sel), cand, tvec)

    return lax.fori_loop(0, 32, bit_step, jnp.full((1, tq), INT_MIN, jnp.int32))


def _softmax_stage(s_ref, p_ref, mask_bias, pv_prev, m_scr, acc_scr):
    lg = s_ref[...] + mask_bias
    m_prev = m_scr[...]
    m_new = jnp.maximum(m_prev, jnp.max(lg, axis=0, keepdims=True))
    alpha = jnp.exp2(m_prev - m_new)
    p_ref[...] = jnp.exp2(lg - m_new).astype(bf16)
    acc_scr[...] = (acc_scr[...] + pv_prev) * alpha
    m_scr[...] = m_new


def _attn_prompt_kernel(qi_ref, wi_ref, q_ref, gate_ref, ki_ref, k_ref, vt_ref, tab_ref, o_ref,
                        key_scr, mb_scr, sa_scr, sb_scr, pa_scr, pb_scr, m_scr, acc_scr, *, k_sel):
    i = pl.program_id(1)
    g = pl.program_id(2)
    tq, tk = ATTN_TQ, ATTN_TK
    masked_tile = mb_scr.shape[0] - 1
    ko = lax.broadcasted_iota(jnp.int32, (tk, tq), 0)
    qo = lax.broadcasted_iota(jnp.int32, (tk, tq), 1)
    causal = (ko // CHUNK) <= (qo // CHUNK)

    @pl.when(g == 0)
    def _select():
        w_t = wi_ref[0] * (D_IDX ** -0.5 * H_IDX ** -0.5)

        def score_tile(t, carry):
            key_scr[t] = _score_keys_km(qi_ref, w_t, ki_ref[0, t])
            return carry

        lax.fori_loop(0, i + 1, score_tile, 0)
        key_scr[i] = jnp.where(causal, key_scr[i], INT_MIN)
        key_scr[masked_tile] = jnp.full((tk, tq), INT_MIN, jnp.int32)
        mb_scr[masked_tile] = jnp.full((tk, tq), NEG, f32)
        tvec = _kth_largest_key_km(key_scr, i + 1, k_sel)

        def mask_tile(t, carry):
            mb_scr[t] = jnp.where(key_scr[t] >= tvec, 0.0, NEG)
            return carry

        lax.fori_loop(0, i + 1, mask_tile, 0)
        mb_scr[i] = jnp.where(causal, mb_scr[i], NEG)

    n_far = jnp.maximum(i - 1, 0)
    n_pairs = (n_far + 1) // 2
    t_prev = jnp.maximum(i - 1, 0)
    mask_prev = jnp.where(i >= 1, i - 1, masked_tile)
    m_scr[...] = jnp.full(m_scr.shape, NEG, f32)
    acc_scr[...] = jnp.zeros(acc_scr.shape, f32)
    pb_scr[...] = jnp.zeros(pb_scr.shape, bf16)
    qs = jnp.concatenate([q_ref[0, :, hh * HEAD_DIM:(hh + 1) * HEAD_DIM] for hh in range(GROUP)], axis=0)

    def qk(t):
        return lax.dot_general(k_ref[0, 0, t], qs, (((1,), (1,)), ((), ())), preferred_element_type=f32)

    def pv(t, p_ref):
        return jnp.dot(vt_ref[0, 0, t], p_ref[...], preferred_element_type=f32)

    def far_mask(t):
        return jnp.concatenate([mb_scr[jnp.where(t < n_far, t, masked_tile)]] * GROUP, axis=1)

    def near_mask(mask_t, a):
        mbt = mb_scr[mask_t]
        return jnp.concatenate([mbt + tab_ref[a, hh] for hh in range(GROUP)], axis=1)

    sa_scr[...] = qk(0)

    def pair_step(tau, carry):
        t0 = 2 * tau
        pv_b = pv(jnp.maximum(t0 - 1, 0), pb_scr)
        sb_scr[...] = qk(jnp.minimum(t0 + 1, n_far - 1))
        _softmax_stage(sa_scr, pa_scr, far_mask(t0), pv_b, m_scr, acc_scr)
        pv_a = pv(t0, pa_scr)
        sa_scr[...] = qk(jnp.minimum(t0 + 2, n_far))
        _softmax_stage(sb_scr, pb_scr, far_mask(t0 + 1), pv_a, m_scr, acc_scr)
        return carry

    lax.fori_loop(0, n_pairs, pair_step, 0)
    pv_b = pv(jnp.clip(2 * n_pairs - 1, 0, jnp.maximum(n_far - 1, 0)), pb_scr)
    sb_scr[...] = qk(i)
    _softmax_stage(sa_scr, pa_scr, near_mask(mask_prev, 1), pv_b, m_scr, acc_scr)
    pv_a = pv(t_prev, pa_scr)
    _softmax_stage(sb_scr, pb_scr, near_mask(i, 0), pv_a, m_scr, acc_scr)
    acc = acc_scr[...] + pv(i, pb_scr)
    o_t = acc[0:HEAD_DIM] / acc[HEAD_DIM:HEAD_DIM + 1]
    for hh in range(GROUP):
        o = o_t[:, hh * tq:(hh + 1) * tq].T
        gt = gate_ref[0, :, hh * HEAD_DIM:(hh + 1) * HEAD_DIM].astype(f32)
        o_ref[0, :, hh * HEAD_DIM:(hh + 1) * HEAD_DIM] = (o * (gt * jax.nn.sigmoid(gt))).astype(o_ref.dtype)


def _attn_prompt(tab, qi_hm, wi_t, q, gate, ki, k_hm, vt_hm):
    b, s, width = q.shape
    tq, tk = ATTN_TQ, ATTN_TK
    nt = s // tk
    nq = s // tq
    k_sel = min(TOPK_MAX, s // 4)
    gw = GROUP * HEAD_DIM
    vrows = HEAD_DIM + ONES_ROWS
    return pl.pallas_call(
        functools.partial(_attn_prompt_kernel, k_sel=k_sel),
        grid=(b, nq, N_KV),
        in_specs=[
            pl.BlockSpec((H_IDX, tq, D_IDX), lambda bb, i, g: (0, bb * nq + i, 0)),
            pl.BlockSpec((1, H_IDX, tq), lambda bb, i, g: (bb, 0, i)),
            pl.BlockSpec((1, tq, gw), lambda bb, i, g: (bb, i, g)),
            pl.BlockSpec((1, tq, gw), lambda bb, i, g: (bb, i, g)),
            pl.BlockSpec((1, nt, tk, D_IDX), lambda bb, i, g: (bb, 0, 0, 0)),
            pl.BlockSpec((1, 1, nt, tk, HEAD_DIM), lambda bb, i, g: (bb, g, 0, 0, 0)),
            pl.BlockSpec((1, 1, nt, vrows, tk), lambda bb, i, g: (bb, g, 0, 0, 0)),
            pl.BlockSpec((2, GROUP, tk, tq), lambda bb, i, g: (0, g, 0, 0)),
        ],
        out_specs=pl.BlockSpec((1, tq, gw), lambda bb, i, g: (bb, i, g)),
        out_shape=jax.ShapeDtypeStruct((b, s, width), bf16),
        scratch_shapes=[
            pltpu.VMEM((nt + 1, tk, tq), jnp.int32),
            pltpu.VMEM((nt + 1, tk, tq), f32),
            pltpu.VMEM((tk, GROUP * tq), f32),
            pltpu.VMEM((tk, GROUP * tq), f32),
            pltpu.VMEM((tk, GROUP * tq), bf16),
            pltpu.VMEM((tk, GROUP * tq), bf16),
            pltpu.VMEM((1, GROUP * tq), f32),
            pltpu.VMEM((vrows, GROUP * tq), f32),
        ],
        compiler_params=_cparams(("parallel", "arbitrary", "arbitrary")),
        name="attn_prompt",
    )(qi_hm, wi_t, q, gate, ki, k_hm, vt_hm, tab)


def _score_keys_qm(qi_ref, w, kt):
    tq = qi_ref.shape[1]
    heads_per_dot = 8
    acc = jnp.zeros((tq, kt.shape[1]), f32)
    for hc in range(H_IDX // heads_per_dot):
        lhs = qi_ref[hc * heads_per_dot:(hc + 1) * heads_per_dot].reshape(heads_per_dot * tq, D_IDX)
        d = jnp.dot(lhs, kt, preferred_element_type=f32)
        for hh in range(heads_per_dot):
            h = hc * heads_per_dot + hh
            acc = acc + jnp.maximum(d[hh * tq:(hh + 1) * tq], 0.0) * w[:, h:h + 1]
    return _float_order_key(acc)


def _kth_largest_key_qm(key_scr, k_sel):
    tq, width = key_scr.shape

    def bit_step(bi, tvec):
        cand = tvec + lax.shift_left(jnp.int32(1), 31 - bi)
        c = jnp.zeros((tq, 128), f32)
        for s in range(width // 128):
            c = c + jnp.where(key_scr[:, s * 128:(s + 1) * 128] >= cand, 1.0, 0.0)
        cnt = jnp.sum(c, axis=1, keepdims=True)
        return jnp.where(cnt >= float(k_sel), cand, tvec)

    return lax.fori_loop(0, 32, bit_step, jnp.full((tq, 1), INT_MIN, jnp.int32))


def _select_sample_kernel(qi_ref, wi_ref, ckit_ref, nkit_ref, mb_ref, key_scr, *, k_sel, n_new):
    tq = qi_ref.shape[1]
    tk = ATTN_TK
    p = ckit_ref.shape[2]
    real = lax.broadcasted_iota(jnp.int32, (tq, tk), 1) < n_new
    w = wi_ref[0] * (D_IDX ** -0.5 * H_IDX ** -0.5)
    for c in range(p // SELECT_TK):
        cols = slice(c * SELECT_TK, (c + 1) * SELECT_TK)
        key_scr[:, cols] = _score_keys_qm(qi_ref, w, ckit_ref[0, :, cols])
    key_scr[:, p:p + tk] = jnp.where(real, _score_keys_qm(qi_ref, w, nkit_ref[0]), INT_MIN)
    tvec = _kth_largest_key_qm(key_scr, k_sel)
    mb_ref[0, :, 0:p] = jnp.where(key_scr[:, 0:p] >= tvec, 0.0, NEG)
    mb_ref[0, :, p:p + tk] = jnp.where(real & (key_scr[:, p:p + tk] >= tvec), 0.0, NEG)


def _select_sample(qi_hm, wi, ckit, nkit, n_new):
    b, t, _ = wi.shape
    p = ckit.shape[2]
    tk = ATTN_TK
    k_sel = min(TOPK_MAX, (p + n_new) // 4)
    return pl.pallas_call(
        functools.partial(_select_sample_kernel, k_sel=k_sel, n_new=n_new),
        grid=(b,),
        in_specs=[
            pl.BlockSpec((H_IDX, t, D_IDX), lambda bb: (0, bb, 0)),
            pl.BlockSpec((1, t, H_IDX), lambda bb: (bb, 0, 0)),
            pl.BlockSpec((1, D_IDX, p), lambda bb: (bb, 0, 0)),
            pl.BlockSpec((1, D_IDX, tk), lambda bb: (bb, 0, 0)),
        ],
        out_specs=pl.BlockSpec((1, t, p + tk), lambda bb: (bb, 0, 0)),
        out_shape=jax.ShapeDtypeStruct((b, t, p + tk), f32),
        scratch_shapes=[pltpu.VMEM((t, p + tk), jnp.int32)],
        compiler_params=_cparams(("parallel",)),
        name="select_sample",
    )(qi_hm, wi, ckit, nkit)


def _attend_tile_qm(q_ref, k_of, v_of, mbt, bias_of, s_scr, p_scr, m_scr, l_scr, acc_scr):
    w = mbt.shape[1]
    mb4 = jnp.concatenate([mbt] * GROUP, axis=0)
    for g in range(N_KV):
        qs = jnp.concatenate(
            [q_ref[0, :, (g * GROUP + hh) * HEAD_DIM:(g * GROUP + hh + 1) * HEAD_DIM] for hh in range(GROUP)], axis=0)
        s_scr[g, :, 0:w] = lax.dot_general(qs, k_of(g), (((1,), (1,)), ((), ())), preferred_element_type=f32)
    for g in range(N_KV):
        lg = s_scr[g, :, 0:w] + mb4
        if bias_of is not None:
            lg = lg + bias_of(g)
        m_prev = m_scr[g]
        m_new = jnp.maximum(m_prev, jnp.max(lg, axis=1, keepdims=True))
        alpha = jnp.exp2(m_prev - m_new)
        p = jnp.exp2(lg - m_new)
        l_scr[g] = alpha * l_scr[g] + jnp.sum(p, axis=1, keepdims=True)
        acc_scr[g] = alpha * acc_scr[g]
        m_scr[g] = m_new
        p_scr[g, :, 0:w] = p.astype(bf16)
    for g in range(N_KV):
        acc_scr[g] = acc_scr[g] + jnp.dot(p_scr[g, :, 0:w], v_of(g), preferred_element_type=f32)


def _attn_sample_kernel(q_ref, gate_ref, ck_ref, cv_ref, nk_ref, nv_ref, mbc_ref, mbn_ref, tab_ref, o_ref,
                        s_scr, p_scr, m_scr, l_scr, acc_scr):
    j = pl.program_id(1)
    nj = pl.num_programs(1)
    t = q_ref.shape[1]
    tk = ATTN_TK
    n_keys = mbc_ref.shape[2]

    @pl.when(j == 0)
    def _init():
        m_scr[...] = jnp.full(m_scr.shape, NEG, f32)
        l_scr[...] = jnp.zeros(l_scr.shape, f32)
        acc_scr[...] = jnp.zeros(acc_scr.shape, f32)

    def head_cols(g):
        return slice(g * HEAD_DIM, (g + 1) * HEAD_DIM)

    def cache_head(ref, g):
        return ref[0, pl.ds(g, n_keys, stride=N_KV), :].astype(bf16)

    def cache_tile(last):
        bias_of = None
        if last:
            far = jnp.zeros((t, n_keys - tk), f32)

            def bias_of(g):
                return jnp.concatenate(
                    [jnp.concatenate([far, tab_ref[1, g * GROUP + hh]], axis=1) for hh in range(GROUP)], axis=0)

        _attend_tile_qm(q_ref, functools.partial(cache_head, ck_ref), functools.partial(cache_head, cv_ref),
                        mbc_ref[0], bias_of, s_scr, p_scr, m_scr, l_scr, acc_scr)

    @pl.when(j < nj - 1)
    def _far():
        cache_tile(False)

    @pl.when(j == nj - 1)
    def _near():
        cache_tile(True)
        _attend_tile_qm(q_ref, lambda g: nk_ref[0, :, head_cols(g)], lambda g: nv_ref[0, :, head_cols(g)],
                        mbn_ref[0],
                        lambda g: jnp.concatenate([tab_ref[0, g * GROUP + hh] for hh in range(GROUP)], axis=0),
                        s_scr, p_scr, m_scr, l_scr, acc_scr)
        for g in range(N_KV):
            o = acc_scr[g] / l_scr[g]
            for hh in range(GROUP):
                hcols = head_cols(g * GROUP + hh)
                gt = gate_ref[0, :, hcols].astype(f32)
                o_ref[0, :, hcols] = (o[hh * t:(hh + 1) * t] * (gt * jax.nn.sigmoid(gt))).astype(o_ref.dtype)


def _attn_sample(tab, mb, q, gate, ck, cv, nk, nv):
    b, t, width = q.shape
    tk = ATTN_TK
    p = ck.shape[1] // N_KV
    nj = p // SAMPLE_TK
    kvw = N_KV * HEAD_DIM
    return pl.pallas_call(
        _attn_sample_kernel,
        grid=(b, nj),
        in_specs=[
            pl.BlockSpec((1, t, width), lambda bb, j: (bb, 0, 0)),
            pl.BlockSpec((1, t, width), lambda bb, j: (bb, 0, 0)),
            pl.BlockSpec((1, SAMPLE_TK * N_KV, HEAD_DIM), lambda bb, j: (bb, j, 0)),
            pl.BlockSpec((1, SAMPLE_TK * N_KV, HEAD_DIM), lambda bb, j: (bb, j, 0)),
            pl.BlockSpec((1, tk, kvw), lambda bb, j: (bb, 0, 0)),
            pl.BlockSpec((1, tk, kvw), lambda bb, j: (bb, 0, 0)),
            pl.BlockSpec((1, t, SAMPLE_TK), lambda bb, j: (bb, 0, j)),
            pl.BlockSpec((1, t, tk), lambda bb, j: (bb, 0, p // tk)),
            pl.BlockSpec((2, N_HEADS, t, tk), lambda bb, j: (0, 0, 0, 0)),
        ],
        out_specs=pl.BlockSpec((1, t, width), lambda bb, j: (bb, 0, 0)),
        out_shape=jax.ShapeDtypeStruct((b, t, width), bf16),
        scratch_shapes=[
            pltpu.VMEM((N_KV, GROUP * t, SAMPLE_TK), f32),
            pltpu.VMEM((N_KV, GROUP * t, SAMPLE_TK), bf16),
            pltpu.VMEM((N_KV, GROUP * t, 1), f32),
            pltpu.VMEM((N_KV, GROUP * t, 1), f32),
            pltpu.VMEM((N_KV, GROUP * t, HEAD_DIM), f32),
        ],
        compiler_params=_cparams(("parallel", "arbitrary")),
        name="attn_sample",
    )(q, gate, ck, cv, nk, nv, mb, mb, tab)


def _pool_kernel(u_ref, hist_ref, gate_ref, gw_ref, scale_ref, o_ref, ext_scr, *, start):
    i = pl.program_id(1)
    tm = u_ref.shape[1]
    gwid = gw_ref.shape[1]

    @pl.when(i == 0)
    def _first():
        ext_scr[0:HALO, :] = hist_ref[0]

    @pl.when(i > 0)
    def _carry():
        ext_scr[0:HALO, :] = ext_scr[tm:tm + HALO, :]

    ext_scr[HALO:HALO + tm, :] = u_ref[0]
    pos = start + i * tm + lax.broadcasted_iota(jnp.int32, (tm, 1), 0)
    for gi, win in enumerate(POOL_WINDOWS):
        lo, hi = gi * gwid, (gi + 1) * gwid
        s = ext_scr[:, lo:hi]
        shift = 1
        while shift < win:
            s = s + pltpu.roll(s, shift, 0)
            shift *= 2
        s = s[HALO:HALO + tm]
        tok = ext_scr[HALO:HALO + tm, lo:hi]
        cnt = jnp.minimum(pos + 1, win).astype(f32)
        mix = (s / cnt - tok).astype(bf16)
        z = jnp.dot(mix, gw_ref[gi], preferred_element_type=f32) * scale_ref[:, lo:hi]
        gt = gate_ref[0, :, lo:hi].astype(f32)
        o_ref[0, :, lo:hi] = (z * (gt * jax.nn.sigmoid(gt))).astype(o_ref.dtype)


def _pool_mix(u, hist16, gate, group_w, scale, start, tm=256):
    b, t, e = u.shape
    tm = min(tm, t)
    gwid = e // N_POOL_GROUPS
    return pl.pallas_call(
        functools.partial(_pool_kernel, start=start),
        grid=(b, t // tm),
        in_specs=[
            pl.BlockSpec((1, tm, e), lambda bb, i: (bb, i, 0)),
            pl.BlockSpec((1, HALO, e), lambda bb, i: (bb, 0, 0)),
            pl.BlockSpec((1, tm, e), lambda bb, i: (bb, i, 0)),
            pl.BlockSpec((N_POOL_GROUPS, gwid, gwid), lambda bb, i: (0, 0, 0)),
            pl.BlockSpec((1, e), lambda bb, i: (0, 0)),
        ],
        out_specs=pl.BlockSpec((1, tm, e), lambda bb, i: (bb, i, 0)),
        out_shape=jax.ShapeDtypeStruct((b, t, e), bf16),
        scratch_shapes=[pltpu.VMEM((HALO + tm, e), f32)],
        compiler_params=_cparams(("parallel", "arbitrary")),
        name="pool_mix",
    )(u, hist16, gate, group_w, scale)


def _attn_weights(w_in):
    qw, kw_ = N_HEADS * HEAD_DIM, N_KV * HEAD_DIM
    o = 0
    parts = {}
    for name, wdt in (("q", qw), ("k", kw_), ("v", kw_), ("qi", H_IDX * D_IDX), ("kiwi", D_IDX + H_IDX),
                      ("gate", qw)):
        parts[name] = w_in[:, o:o + wdt].astype(bf16)
        o += wdt
    parts["kiwi"] = jnp.pad(parts["kiwi"], ((0, 0), (0, 128 - (D_IDX + H_IDX))))
    return parts


def _attn_project(h, wts, b, t):
    q = _matmul(h, wts["q"], bf16, scale=SM_SCALE_LOG2, tn=WIDE_TN).reshape(b, t, -1)
    k = _matmul(h, wts["k"], f32).reshape(b, t, N_KV, HEAD_DIM)
    v = _matmul(h, wts["v"], f32).reshape(b, t, N_KV, HEAD_DIM)
    qi_hm = _matmul_heads(h, wts["qi"], D_IDX, bf16)
    kiwi = _matmul(h, wts["kiwi"], f32).reshape(b, t, 128)
    gate = _matmul(h, wts["gate"], bf16, tn=WIDE_TN).reshape(b, t, -1)
    ki = kiwi[..., :D_IDX]
    wi = kiwi[..., D_IDX:D_IDX + H_IDX]
    return q, k, v, qi_hm, ki, wi, gate


def _pad_rows(x, rows):
    return jnp.pad(x, ((0, 0), (0, rows - x.shape[1])) + ((0, 0),) * (x.ndim - 2))


def kernel(x_prompt, x_sample, cache_k, cache_v, cache_kidx, state_pool, norm_w, final_norm_w,
           attn_w_in, attn_w_out, rel_bias, pool_w_in, pool_group_w, pool_scale, pool_w_out):
    bp, sp, d = x_prompt.shape
    bs, ts, _ = x_sample.shape
    past = cache_k.shape[2]
    assert ATTN_TQ == ATTN_TK and ATTN_TK >= MAX_DISTANCE and SAMPLE_TK % ATTN_TK == 0
    assert sp % ATTN_TQ == 0 and past % SAMPLE_TK == 0 and past % SELECT_TK == 0
    assert past % CHUNK == 0 and ts <= CHUNK and ts % 16 == 0 and ts >= POOL_HIST
    depth = norm_w.shape[0]

    xp = x_prompt.reshape(bp * sp, d)
    xs = x_sample.reshape(bs * ts, d)
    tab_km = _bias_tables(rel_bias, True, ATTN_TQ)
    tab_qm = _bias_tables(rel_bias, False, ts)
    outs = {name: [] for name in ("kp", "vp", "kip", "poolp", "ks", "vs", "kis", "pools")}

    for layer in range(depth):
        hp = _rmsnorm(xp, norm_w[layer], bf16)
        hs = _rmsnorm(xs, norm_w[layer], bf16)
        if layer % 2 == 0:
            a = layer // 2
            wts = _attn_weights(attn_w_in[a])
            w_out = attn_w_out[a].astype(bf16)

            q, k, v, qi_hm, ki, wi, gate = _attn_project(hp, wts, bp, sp)
            nt = sp // ATTN_TK
            k_hm = jnp.transpose(k.astype(bf16), (0, 2, 1, 3)).reshape(bp, N_KV, nt, ATTN_TK, HEAD_DIM)
            vt_hm = jnp.transpose(v.astype(bf16).reshape(bp, nt, ATTN_TK, N_KV, HEAD_DIM), (0, 3, 1, 4, 2))
            vt_hm = jnp.concatenate([vt_hm, jnp.ones((bp, N_KV, nt, ONES_ROWS, ATTN_TK), bf16)], axis=3)
            og = _attn_prompt(tab_km, qi_hm, jnp.transpose(wi, (0, 2, 1)), q, gate,
                              ki.astype(bf16).reshape(bp, nt, ATTN_TK, D_IDX), k_hm, vt_hm)
            xp = _matmul(og.reshape(bp * sp, -1), w_out, f32, res=xp)
            outs["kp"].append(k); outs["vp"].append(v); outs["kip"].append(ki)

            q, k, v, qi_hm, ki, wi, gate = _attn_project(hs, wts, bs, ts)
            ckit = jnp.transpose(cache_kidx[a].astype(bf16), (0, 2, 1))
            nkit = jnp.transpose(_pad_rows(ki.astype(bf16), ATTN_TK), (0, 2, 1))
            nk = _pad_rows(k.astype(bf16).reshape(bs, ts, -1), ATTN_TK)
            nv = _pad_rows(v.astype(bf16).reshape(bs, ts, -1), ATTN_TK)
            mb = _select_sample(qi_hm, wi, ckit, nkit, ts)
            og = _attn_sample(tab_qm, mb, q, gate, cache_k[a].reshape(bs, past * N_KV, HEAD_DIM),
                              cache_v[a].reshape(bs, past * N_KV, HEAD_DIM), nk, nv)
            xs = _matmul(og.reshape(bs * ts, -1), w_out, f32, res=xs)
            outs["ks"].append(k); outs["vs"].append(v); outs["kis"].append(ki)
        else:
            p = layer // 2
            e = pool_w_in.shape[2] // 2
            w_u = pool_w_in[p][:, :e].astype(bf16)
            w_g = pool_w_in[p][:, e:].astype(bf16)
            gw = pool_group_w[p].astype(bf16)
            scale = pool_scale[p].reshape(1, e)
            w_out = pool_w_out[p].astype(bf16)

            u = _matmul(hp, w_u, f32, tn=WIDE_TN).reshape(bp, sp, e)
            gate = _matmul(hp, w_g, bf16, tn=WIDE_TN).reshape(bp, sp, e)
            zg = _pool_mix(u, jnp.zeros((bp, HALO, e), f32), gate, gw, scale, 0)
            xp = _matmul(zg.reshape(bp * sp, e), w_out, f32, res=xp)
            outs["poolp"].append(u[:, sp - POOL_HIST:])

            u = _matmul(hs, w_u, f32, tn=WIDE_TN).reshape(bs, ts, e)
            gate = _matmul(hs, w_g, bf16, tn=WIDE_TN).reshape(bs, ts, e)
            hist16 = jnp.pad(state_pool[p], ((0, 0), (HALO - POOL_HIST, 0), (0, 0)))
            zg = _pool_mix(u, hist16, gate, gw, scale, past)
            xs = _matmul(zg.reshape(bs * ts, e), w_out, f32, res=xs)
            outs["pools"].append(u[:, ts - POOL_HIST:])

    y_prompt = _rmsnorm(xp, final_norm_w, f32).reshape(bp, sp, d)
    y_sample = _rmsnorm(xs, final_norm_w, f32).reshape(bs, ts, d)
    return (y_prompt, y_sample, jnp.stack(outs["kp"]), jnp.stack(outs["vp"]), jnp.stack(outs["kip"]),
            jnp.stack(outs["poolp"]), jnp.stack(outs["ks"]), jnp.stack(outs["vs"]), jnp.stack(outs["kis"]),
            jnp.stack(outs["pools"]))
```

```python
import functools
import math

import jax
import jax.numpy as jnp
from jax import lax
from jax.experimental import pallas as pl
from jax.experimental.pallas import tpu as pltpu

CHUNK = 64
N_HEADS = 32
HEAD_DIM = 128
N_KV = 8
GROUP = N_HEADS // N_KV
H_IDX = 32
D_IDX = 64
TOPK_MAX = 256
NUM_BUCKETS = 32
MAX_DISTANCE = 128
FAR_BUCKET = NUM_BUCKETS // 2 - 1
POOL_WINDOWS = (2, 4, 8, 16)
N_POOL_GROUPS = len(POOL_WINDOWS)
POOL_HIST = max(POOL_WINDOWS) - 1
EPS = 1e-6

ATTN_TQ = 256
ATTN_TK = 256
SAMPLE_TK = 1024
WIDE_TN = 1024
SELECT_TK = 1024
ONES_ROWS = 16
HALO = 16
VMEM_LIMIT = 52 * 1024 * 1024

f32 = jnp.float32
bf16 = jnp.bfloat16
INT_MIN = -(2 ** 31)
NEG = -1e30
LOG2E = math.log2(math.e)
SM_SCALE_LOG2 = HEAD_DIM ** -0.5 * LOG2E


def _cparams(sem):
    return pltpu.CompilerParams(dimension_semantics=sem, vmem_limit_bytes=VMEM_LIMIT)


def _rmsnorm_kernel(x_ref, w_ref, o_ref):
    x = x_ref[...]
    y = x * lax.rsqrt(jnp.mean(x * x, axis=-1, keepdims=True) + EPS)
    o_ref[...] = (y * w_ref[...]).astype(o_ref.dtype)


def _rmsnorm(x, w, out_dtype, tm=512):
    m, d = x.shape
    tm = min(tm, m)
    assert m % tm == 0
    return pl.pallas_call(
        _rmsnorm_kernel,
        grid=(m // tm,),
        in_specs=[pl.BlockSpec((tm, d), lambda i: (i, 0)), pl.BlockSpec((1, d), lambda i: (0, 0))],
        out_specs=pl.BlockSpec((tm, d), lambda i: (i, 0)),
        out_shape=jax.ShapeDtypeStruct((m, d), out_dtype),
        compiler_params=_cparams(("parallel",)),
        name="rmsnorm",
    )(x, w.reshape(1, d))


def _mm_kernel(a_ref, w_ref, o_ref, *, scale):
    acc = jnp.dot(a_ref[...], w_ref[...], preferred_element_type=f32)
    if scale is not None:
        acc = acc * scale
    o_ref[...] = acc.astype(o_ref.dtype)


def _mm_res_kernel(a_ref, w_ref, r_ref, o_ref):
    acc = jnp.dot(a_ref[...], w_ref[...], preferred_element_type=f32)
    o_ref[...] = (r_ref[...] + acc).astype(o_ref.dtype)


def _matmul(a, w, out_dtype, res=None, scale=None, cols=None, tm=1024, tn=512):
    m, k = a.shape
    c0, n = (0, w.shape[1]) if cols is None else cols
    tm = min(tm, m)
    tn = min(tn, n)
    assert m % tm == 0 and n % tn == 0 and c0 % tn == 0
    j0 = c0 // tn
    in_specs = [pl.BlockSpec((tm, k), lambda i, j: (i, 0)), pl.BlockSpec((k, tn), lambda i, j: (0, j + j0))]
    args = [a, w]
    kern = functools.partial(_mm_kernel, scale=scale)
    if res is not None:
        assert scale is None
        in_specs.append(pl.BlockSpec((tm, tn), lambda i, j: (i, j)))
        args.append(res)
        kern = _mm_res_kernel
    return pl.pallas_call(
        kern,
        grid=(m // tm, n // tn),
        in_specs=in_specs,
        out_specs=pl.BlockSpec((tm, tn), lambda i, j: (i, j)),
        out_shape=jax.ShapeDtypeStruct((m, n), out_dtype),
        compiler_params=_cparams(("parallel", "parallel")),
        name="matmul_res" if res is not None else "matmul",
    )(*args)


def _mm_heads_kernel(a_ref, w_ref, o_ref):
    acc = jnp.dot(a_ref[...], w_ref[...], preferred_element_type=f32)
    dh = o_ref.shape[2]
    for hh in range(o_ref.shape[0]):
        o_ref[hh] = acc[:, hh * dh:(hh + 1) * dh].astype(o_ref.dtype)


def _matmul_heads(a, w, cols, dh, out_dtype, tm=1024, tn=512):
    m, k = a.shape
    c0, n = cols
    tm = min(tm, m)
    assert m % tm == 0 and n % tn == 0 and tn % dh == 0 and c0 % tn == 0
    j0 = c0 // tn
    return pl.pallas_call(
        _mm_heads_kernel,
        grid=(m // tm, n // tn),
        in_specs=[pl.BlockSpec((tm, k), lambda i, j: (i, 0)), pl.BlockSpec((k, tn), lambda i, j: (0, j + j0))],
        out_specs=pl.BlockSpec((tn // dh, tm, dh), lambda i, j: (j, i, 0)),
        out_shape=jax.ShapeDtypeStruct((n // dh, m, dh), out_dtype),
        compiler_params=_cparams(("parallel", "parallel")),
        name="matmul_heads",
    )(a, w)


def _rel_bucket(rel):
    nb = NUM_BUCKETS // 2
    ret = jnp.where(rel > 0, nb, 0)
    n = jnp.abs(rel)
    max_exact = nb // 2
    nf = jnp.maximum(n, 1).astype(f32)
    large = max_exact + (jnp.log(nf / max_exact) / math.log(MAX_DISTANCE / max_exact)
                         * (nb - max_exact)).astype(jnp.int32)
    large = jnp.minimum(large, nb - 1)
    return ret + jnp.where(n < max_exact, n, large)


def _bias_table_kernel(rb_ref, bk_ref, o_ref):
    bk = bk_ref[0]

    def head(h, carry):
        acc = jnp.zeros(bk.shape, f32)
        for b in range(NUM_BUCKETS):
            acc = jnp.where(bk == b, rb_ref[b, h], acc)
        o_ref[0, h] = (acc - rb_ref[FAR_BUCKET, h]) * LOG2E
        return carry

    lax.fori_loop(0, N_HEADS, head, 0)


def _bias_tables(rel_bias, key_major, n_queries):
    shape = (ATTN_TK, n_queries) if key_major else (n_queries, ATTN_TK)
    ax_q, ax_k = (1, 0) if key_major else (0, 1)
    qo = lax.broadcasted_iota(jnp.int32, shape, ax_q)
    ko = lax.broadcasted_iota(jnp.int32, shape, ax_k)
    buckets = jnp.stack([_rel_bucket(ko - qo), _rel_bucket(ko - ATTN_TK - qo)])
    return pl.pallas_call(
        _bias_table_kernel,
        grid=(2,),
        in_specs=[pl.BlockSpec(memory_space=pltpu.SMEM),
                  pl.BlockSpec((1,) + shape, lambda a: (a, 0, 0))],
        out_specs=pl.BlockSpec((1, N_HEADS) + shape, lambda a: (a, 0, 0, 0)),
        out_shape=jax.ShapeDtypeStruct((2, N_HEADS) + shape, f32),
        compiler_params=_cparams(("parallel",)),
        name="bias_tables",
    )(rel_bias, buckets)


def _float_order_key(x):
    bits = pltpu.bitcast(x + 0.0, jnp.int32)
    return bits ^ ((bits >> 31) & jnp.int32(0x7FFFFFFF))


def _score_keys_km(qi_ref, w_t, ki_t):
    tq = qi_ref.shape[1]
    heads_per_dot = 8
    acc = jnp.zeros((ki_t.shape[0], tq), f32)
    for hc in range(H_IDX // heads_per_dot):
        rhs = qi_ref[hc * heads_per_dot:(hc + 1) * heads_per_dot].reshape(heads_per_dot * tq, D_IDX)
        d = lax.dot_general(ki_t, rhs, (((1,), (1,)), ((), ())), preferred_element_type=f32)
        for hh in range(heads_per_dot):
            h = hc * heads_per_dot + hh
            acc = acc + jnp.maximum(d[:, hh * tq:(hh + 1) * tq], 0.0) * w_t[h:h + 1, :]
    return _float_order_key(acc)


def _kth_largest_key_km(key_scr, n_tiles, k_sel):
    tk, tq = key_scr.shape[1], key_scr.shape[2]
    empty_tile = key_scr.shape[0] - 1

    def bit_step(bi, tvec):
        cand = tvec + lax.shift_left(jnp.int32(1), 31 - bi)

        def count(t, c):
            one = jnp.where(key_scr[t] >= cand, 1.0, 0.0)
            return c + jnp.sum(one.reshape(tk // 8, 8, tq), axis=0)

        def count_pair(tau, c):
            t1 = 2 * tau + 1
            return count(jnp.where(t1 < n_tiles, t1, empty_tile), count(2 * tau, c))

        c = lax.fori_loop(0, (n_tiles + 1) // 2, count_pair, jnp.zeros((8, tq), f32))
        cnt = jnp.sum(c, axis=0, keepdims=True)
        return jnp.where(cnt >= float(k_sel), cand, tvec)

    return lax.fori_loop(0, 32, bit_step, jnp.full((1, tq), INT_MIN, jnp.int32))


def _softmax_stage(s_ref, p_ref, mask_bias, pv_prev, m_scr, acc_scr):
    lg = s_ref[...] + mask_bias
    m_prev = m_scr[...]
    m_new = jnp.maximum(m_prev, jnp.max(lg, axis=0, keepdims=True))
    alpha = jnp.exp2(m_prev - m_new)
    p_ref[...] = jnp.exp2(lg - m_new).astype(bf16)
    acc_scr[...] = (acc_scr[...] + pv_prev) * alpha
    m_scr[...] = m_new


def _attn_prompt_kernel(qi_ref, wi_ref, q_ref, gate_ref, ki_ref, k_ref, vt_ref, tab_ref, o_ref,
                        key_scr, mb_scr, sa_scr, sb_scr, pa_scr, pb_scr, m_scr, acc_scr, *, k_sel):
    i = pl.program_id(1)
    g = pl.program_id(2)
    tq, tk = ATTN_TQ, ATTN_TK
    masked_tile = mb_scr.shape[0] - 1
    ko = lax.broadcasted_iota(jnp.int32, (tk, tq), 0)
    qo = lax.broadcasted_iota(jnp.int32, (tk, tq), 1)
    causal = (ko // CHUNK) <= (qo // CHUNK)

    @pl.when(g == 0)
    def _select():
        w_t = wi_ref[0] * (D_IDX ** -0.5 * H_IDX ** -0.5)

        def score_tile(t, carry):
            key_scr[t] = _score_keys_km(qi_ref, w_t, ki_ref[0, t])
            return carry

        lax.fori_loop(0, i + 1, score_tile, 0)
        key_scr[i] = jnp.where(causal, key_scr[i], INT_MIN)
        key_scr[masked_tile] = jnp.full((tk, tq), INT_MIN, jnp.int32)
        mb_scr[masked_tile] = jnp.full((tk, tq), NEG, f32)
        tvec = _kth_largest_key_km(key_scr, i + 1, k_sel)

        def mask_tile(t, carry):
            mb_scr[t] = jnp.where(key_scr[t] >= tvec, 0.0, NEG)
            return carry

        lax.fori_loop(0, i + 1, mask_tile, 0)
        mb_scr[i] = jnp.where(causal, mb_scr[i], NEG)

    n_far = jnp.maximum(i - 1, 0)
    n_pairs = (n_far + 1) // 2
    t_prev = jnp.maximum(i - 1, 0)
    mask_prev = jnp.where(i >= 1, i - 1, masked_tile)
    m_scr[...] = jnp.full(m_scr.shape, NEG, f32)
    acc_scr[...] = jnp.zeros(acc_scr.shape, f32)
    pb_scr[...] = jnp.zeros(pb_scr.shape, bf16)
    qs = jnp.concatenate([q_ref[0, :, hh * HEAD_DIM:(hh + 1) * HEAD_DIM] for hh in range(GROUP)], axis=0)

    def qk(t):
        return lax.dot_general(k_ref[0, 0, t], qs, (((1,), (1,)), ((), ())), preferred_element_type=f32)

    def pv(t, p_ref):
        return jnp.dot(vt_ref[0, 0, t], p_ref[...], preferred_element_type=f32)

    def far_mask(t):
        return jnp.concatenate([mb_scr[jnp.where(t < n_far, t, masked_tile)]] * GROUP, axis=1)

    def near_mask(mask_t, a):
        mbt = mb_scr[mask_t]
        return jnp.concatenate([mbt + tab_ref[a, hh] for hh in range(GROUP)], axis=1)

    sa_scr[...] = qk(0)

    def pair_step(tau, carry):
        t0 = 2 * tau
        pv_b = pv(jnp.maximum(t0 - 1, 0), pb_scr)
        sb_scr[...] = qk(jnp.minimum(t0 + 1, n_far - 1))
        _softmax_stage(sa_scr, pa_scr, far_mask(t0), pv_b, m_scr, acc_scr)
        pv_a = pv(t0, pa_scr)
        sa_scr[...] = qk(jnp.minimum(t0 + 2, n_far))
        _softmax_stage(sb_scr, pb_scr, far_mask(t0 + 1), pv_a, m_scr, acc_scr)
        return carry

    lax.fori_loop(0, n_pairs, pair_step, 0)
    pv_b = pv(jnp.clip(2 * n_pairs - 1, 0, jnp.maximum(n_far - 1, 0)), pb_scr)
    sb_scr[...] = qk(i)
    _softmax_stage(sa_scr, pa_scr, near_mask(mask_prev, 1), pv_b, m_scr, acc_scr)
    pv_a = pv(t_prev, pa_scr)
    _softmax_stage(sb_scr, pb_scr, near_mask(i, 0), pv_a, m_scr, acc_scr)
    acc = acc_scr[...] + pv(i, pb_scr)
    o_t = acc[0:HEAD_DIM] / acc[HEAD_DIM:HEAD_DIM + 1]
    for hh in range(GROUP):
        o = o_t[:, hh * tq:(hh + 1) * tq].T
        gt = gate_ref[0, :, hh * HEAD_DIM:(hh + 1) * HEAD_DIM].astype(f32)
        o_ref[0, :, hh * HEAD_DIM:(hh + 1) * HEAD_DIM] = (o * (gt * jax.nn.sigmoid(gt))).astype(o_ref.dtype)


def _attn_prompt(tab, qi_hm, wi_t, q, gate, ki, k_hm, vt_hm):
    b, s, width = q.shape
    tq, tk = ATTN_TQ, ATTN_TK
    nt = s // tk
    nq = s // tq
    k_sel = min(TOPK_MAX, s // 4)
    gw = GROUP * HEAD_DIM
    vrows = HEAD_DIM + ONES_ROWS
    return pl.pallas_call(
        functools.partial(_attn_prompt_kernel, k_sel=k_sel),
        grid=(b, nq, N_KV),
        in_specs=[
            pl.BlockSpec((H_IDX, tq, D_IDX), lambda bb, i, g: (0, bb * nq + i, 0)),
            pl.BlockSpec((1, H_IDX, tq), lambda bb, i, g: (bb, 0, i)),
            pl.BlockSpec((1, tq, gw), lambda bb, i, g: (bb, i, g)),
            pl.BlockSpec((1, tq, gw), lambda bb, i, g: (bb, i, g)),
            pl.BlockSpec((1, nt, tk, D_IDX), lambda bb, i, g: (bb, 0, 0, 0)),
            pl.BlockSpec((1, 1, nt, tk, HEAD_DIM), lambda bb, i, g: (bb, g, 0, 0, 0)),
            pl.BlockSpec((1, 1, nt, vrows, tk), lambda bb, i, g: (bb, g, 0, 0, 0)),
            pl.BlockSpec((2, GROUP, tk, tq), lambda bb, i, g: (0, g, 0, 0)),
        ],
        out_specs=pl.BlockSpec((1, tq, gw), lambda bb, i, g: (bb, i, g)),
        out_shape=jax.ShapeDtypeStruct((b, s, width), bf16),
        scratch_shapes=[
            pltpu.VMEM((nt + 1, tk, tq), jnp.int32),
            pltpu.VMEM((nt + 1, tk, tq), f32),
            pltpu.VMEM((tk, GROUP * tq), f32),
            pltpu.VMEM((tk, GROUP * tq), f32),
            pltpu.VMEM((tk, GROUP * tq), bf16),
            pltpu.VMEM((tk, GROUP * tq), bf16),
            pltpu.VMEM((1, GROUP * tq), f32),
            pltpu.VMEM((vrows, GROUP * tq), f32),
        ],
        compiler_params=_cparams(("parallel", "arbitrary", "arbitrary")),
        name="attn_prompt",
    )(qi_hm, wi_t, q, gate, ki, k_hm, vt_hm, tab)


def _score_keys_qm(qi_ref, w, kt):
    tq = qi_ref.shape[1]
    heads_per_dot = 8
    acc = jnp.zeros((tq, kt.shape[1]), f32)
    for hc in range(H_IDX // heads_per_dot):
        lhs = qi_ref[hc * heads_per_dot:(hc + 1) * heads_per_dot].reshape(heads_per_dot * tq, D_IDX)
        d = jnp.dot(lhs, kt, preferred_element_type=f32)
        for hh in range(heads_per_dot):
            h = hc * heads_per_dot + hh
            acc = acc + jnp.maximum(d[hh * tq:(hh + 1) * tq], 0.0) * w[:, h:h + 1]
    return _float_order_key(acc)


def _kth_largest_key_qm(key_scr, k_sel):
    tq, width = key_scr.shape

    def bit_step(bi, tvec):
        cand = tvec + lax.shift_left(jnp.int32(1), 31 - bi)
        c = jnp.zeros((tq, 128), f32)
        for s in range(width // 128):
            c = c + jnp.where(key_scr[:, s * 128:(s + 1) * 128] >= cand, 1.0, 0.0)
        cnt = jnp.sum(c, axis=1, keepdims=True)
        return jnp.where(cnt >= float(k_sel), cand, tvec)

    return lax.fori_loop(0, 32, bit_step, jnp.full((tq, 1), INT_MIN, jnp.int32))


def _select_sample_kernel(qi_ref, wi_ref, ckit_ref, nkit_ref, mb_ref, key_scr, *, k_sel, n_new):
    tq = qi_ref.shape[1]
    tk = ATTN_TK
    p = ckit_ref.shape[2]
    real = lax.broadcasted_iota(jnp.int32, (tq, tk), 1) < n_new
    w = wi_ref[0] * (D_IDX ** -0.5 * H_IDX ** -0.5)
    for c in range(p // SELECT_TK):
        cols = slice(c * SELECT_TK, (c + 1) * SELECT_TK)
        key_scr[:, cols] = _score_keys_qm(qi_ref, w, ckit_ref[0, :, cols])
    key_scr[:, p:p + tk] = jnp.where(real, _score_keys_qm(qi_ref, w, nkit_ref[0]), INT_MIN)
    tvec = _kth_largest_key_qm(key_scr, k_sel)
    mb_ref[0, :, 0:p] = jnp.where(key_scr[:, 0:p] >= tvec, 0.0, NEG)
    mb_ref[0, :, p:p + tk] = jnp.where(real & (key_scr[:, p:p + tk] >= tvec), 0.0, NEG)


def _select_sample(qi_hm, wi, ckit, nkit, n_new):
    b, t, _ = wi.shape
    p = ckit.shape[2]
    tk = ATTN_TK
    k_sel = min(TOPK_MAX, (p + n_new) // 4)
    return pl.pallas_call(
        functools.partial(_select_sample_kernel, k_sel=k_sel, n_new=n_new),
        grid=(b,),
        in_specs=[
            pl.BlockSpec((H_IDX, t, D_IDX), lambda bb: (0, bb, 0)),
            pl.BlockSpec((1, t, H_IDX), lambda bb: (bb, 0, 0)),
            pl.BlockSpec((1, D_IDX, p), lambda bb: (bb, 0, 0)),
            pl.BlockSpec((1, D_IDX, tk), lambda bb: (bb, 0, 0)),
        ],
        out_specs=pl.BlockSpec((1, t, p + tk), lambda bb: (bb, 0, 0)),
        out_shape=jax.ShapeDtypeStruct((b, t, p + tk), f32),
        scratch_shapes=[pltpu.VMEM((t, p + tk), jnp.int32)],
        compiler_params=_cparams(("parallel",)),
        name="select_sample",
    )(qi_hm, wi, ckit, nkit)


def _attend_tile_qm(q_ref, k_of, v_of, mbt, bias_of, s_scr, p_scr, m_scr, l_scr, acc_scr):
    w = mbt.shape[1]
    mb4 = jnp.concatenate([mbt] * GROUP, axis=0)
    for g in range(N_KV):
        qs = jnp.concatenate(
            [q_ref[0, :, (g * GROUP + hh) * HEAD_DIM:(g * GROUP + hh + 1) * HEAD_DIM] for hh in range(GROUP)], axis=0)
        s_scr[g, :, 0:w] = lax.dot_general(qs, k_of(g), (((1,), (1,)), ((), ())), preferred_element_type=f32)
    for g in range(N_KV):
        lg = s_scr[g, :, 0:w] + mb4
        if bias_of is not None:
            lg = lg + bias_of(g)
        m_prev = m_scr[g]
        m_new = jnp.maximum(m_prev, jnp.max(lg, axis=1, keepdims=True))
        alpha = jnp.exp2(m_prev - m_new)
        p = jnp.exp2(lg - m_new)
        l_scr[g] = alpha * l_scr[g] + jnp.sum(p, axis=1, keepdims=True)
        acc_scr[g] = alpha * acc_scr[g]
        m_scr[g] = m_new
        p_scr[g, :, 0:w] = p.astype(bf16)
    for g in range(N_KV):
        acc_scr[g] = acc_scr[g] + jnp.dot(p_scr[g, :, 0:w], v_of(g), preferred_element_type=f32)


def _attn_sample_kernel(q_ref, gate_ref, ck_ref, cv_ref, nk_ref, nv_ref, mbc_ref, mbn_ref, tab_ref, o_ref,
                        s_scr, p_scr, m_scr, l_scr, acc_scr):
    j = pl.program_id(1)
    nj = pl.num_programs(1)
    t = q_ref.shape[1]
    tk = ATTN_TK
    n_keys = mbc_ref.shape[2]

    @pl.when(j == 0)
    def _init():
        m_scr[...] = jnp.full(m_scr.shape, NEG, f32)
        l_scr[...] = jnp.zeros(l_scr.shape, f32)
        acc_scr[...] = jnp.zeros(acc_scr.shape, f32)

    def head_cols(g):
        return slice(g * HEAD_DIM, (g + 1) * HEAD_DIM)

    def cache_head(ref, g):
        return ref[0, pl.ds(g, n_keys, stride=N_KV), :].astype(bf16)

    def cache_tile(last):
        bias_of = None
        if last:
            far = jnp.zeros((t, n_keys - tk), f32)

            def bias_of(g):
                return jnp.concatenate(
                    [jnp.concatenate([far, tab_ref[1, g * GROUP + hh]], axis=1) for hh in range(GROUP)], axis=0)

        _attend_tile_qm(q_ref, functools.partial(cache_head, ck_ref), functools.partial(cache_head, cv_ref),
                        mbc_ref[0], bias_of, s_scr, p_scr, m_scr, l_scr, acc_scr)

    @pl.when(j < nj - 1)
    def _far():
        cache_tile(False)

    @pl.when(j == nj - 1)
    def _near():
        cache_tile(True)
        _attend_tile_qm(q_ref, lambda g: nk_ref[0, :, head_cols(g)], lambda g: nv_ref[0, :, head_cols(g)],
                        mbn_ref[0],
                        lambda g: jnp.concatenate([tab_ref[0, g * GROUP + hh] for hh in range(GROUP)], axis=0),
                        s_scr, p_scr, m_scr, l_scr, acc_scr)
        for g in range(N_KV):
            o = acc_scr[g] / l_scr[g]
            for hh in range(GROUP):
                hcols = head_cols(g * GROUP + hh)
                gt = gate_ref[0, :, hcols].astype(f32)
                o_ref[0, :, hcols] = (o[hh * t:(hh + 1) * t] * (gt * jax.nn.sigmoid(gt))).astype(o_ref.dtype)


def _attn_sample(tab, mb, q, gate, ck, cv, nk, nv):
    b, t, width = q.shape
    tk = ATTN_TK
    p = ck.shape[1] // N_KV
    nj = p // SAMPLE_TK
    kvw = N_KV * HEAD_DIM
    return pl.pallas_call(
        _attn_sample_kernel,
        grid=(b, nj),
        in_specs=[
            pl.BlockSpec((1, t, width), lambda bb, j: (bb, 0, 0)),
            pl.BlockSpec((1, t, width), lambda bb, j: (bb, 0, 0)),
            pl.BlockSpec((1, SAMPLE_TK * N_KV, HEAD_DIM), lambda bb, j: (bb, j, 0)),
            pl.BlockSpec((1, SAMPLE_TK * N_KV, HEAD_DIM), lambda bb, j: (bb, j, 0)),
            pl.BlockSpec((1, tk, kvw), lambda bb, j: (bb, 0, 0)),
            pl.BlockSpec((1, tk, kvw), lambda bb, j: (bb, 0, 0)),
            pl.BlockSpec((1, t, SAMPLE_TK), lambda bb, j: (bb, 0, j)),
            pl.BlockSpec((1, t, tk), lambda bb, j: (bb, 0, p // tk)),
            pl.BlockSpec((2, N_HEADS, t, tk), lambda bb, j: (0, 0, 0, 0)),
        ],
        out_specs=pl.BlockSpec((1, t, width), lambda bb, j: (bb, 0, 0)),
        out_shape=jax.ShapeDtypeStruct((b, t, width), bf16),
        scratch_shapes=[
            pltpu.VMEM((N_KV, GROUP * t, SAMPLE_TK), f32),
            pltpu.VMEM((N_KV, GROUP * t, SAMPLE_TK), bf16),
            pltpu.VMEM((N_KV, GROUP * t, 1), f32),
            pltpu.VMEM((N_KV, GROUP * t, 1), f32),
            pltpu.VMEM((N_KV, GROUP * t, HEAD_DIM), f32),
        ],
        compiler_params=_cparams(("parallel", "arbitrary")),
        name="attn_sample",
    )(q, gate, ck, cv, nk, nv, mb, mb, tab)


def _pool_kernel(u_ref, hist_ref, gate_ref, gw_ref, scale_ref, o_ref, ext_scr, *, start):
    i = pl.program_id(1)
    tm = u_ref.shape[1]
    gwid = gw_ref.shape[1]

    @pl.when(i == 0)
    def _first():
        ext_scr[0:HALO, :] = hist_ref[0]

    @pl.when(i > 0)
    def _carry():
        ext_scr[0:HALO, :] = ext_scr[tm:tm + HALO, :]

    ext_scr[HALO:HALO + tm, :] = u_ref[0]
    pos = start + i * tm + lax.broadcasted_iota(jnp.int32, (tm, 1), 0)
    for gi, win in enumerate(POOL_WINDOWS):
        lo, hi = gi * gwid, (gi + 1) * gwid
        s = ext_scr[:, lo:hi]
        shift = 1
        while shift < win:
            s = s + pltpu.roll(s, shift, 0)
            shift *= 2
        s = s[HALO:HALO + tm]
        tok = ext_scr[HALO:HALO + tm, lo:hi]
        cnt = jnp.minimum(pos + 1, win).astype(f32)
        mix = (s / cnt - tok).astype(bf16)
        z = jnp.dot(mix, gw_ref[gi], preferred_element_type=f32) * scale_ref[:, lo:hi]
        gt = gate_ref[0, :, lo:hi].astype(f32)
        o_ref[0, :, lo:hi] = (z * (gt * jax.nn.sigmoid(gt))).astype(o_ref.dtype)


def _pool_mix(u, hist16, gate, group_w, scale, start, tm=256):
    b, t, e = u.shape
    tm = min(tm, t)
    gwid = e // N_POOL_GROUPS
    return pl.pallas_call(
        functools.partial(_pool_kernel, start=start),
        grid=(b, t // tm),
        in_specs=[
            pl.BlockSpec((1, tm, e), lambda bb, i: (bb, i, 0)),
            pl.BlockSpec((1, HALO, e), lambda bb, i: (bb, 0, 0)),
            pl.BlockSpec((1, tm, e), lambda bb, i: (bb, i, 0)),
            pl.BlockSpec((N_POOL_GROUPS, gwid, gwid), lambda bb, i: (0, 0, 0)),
            pl.BlockSpec((1, e), lambda bb, i: (0, 0)),
        ],
        out_specs=pl.BlockSpec((1, tm, e), lambda bb, i: (bb, i, 0)),
        out_shape=jax.ShapeDtypeStruct((b, t, e), bf16),
        scratch_shapes=[pltpu.VMEM((HALO + tm, e), f32)],
        compiler_params=_cparams(("parallel", "arbitrary")),
        name="pool_mix",
    )(u, hist16, gate, group_w, scale)


Q_COLS = N_HEADS * HEAD_DIM
KV_COLS = N_KV * HEAD_DIM
QI_COLS = H_IDX * D_IDX
OFF_K = Q_COLS
OFF_V = OFF_K + KV_COLS
OFF_QI = OFF_V + KV_COLS
OFF_KIWI = OFF_QI + QI_COLS
OFF_GATE = OFF_KIWI + D_IDX + H_IDX


def _attn_weights(w_in):
    w_all = w_in.astype(bf16)
    w_kiwi = jnp.pad(w_all[:, OFF_KIWI:OFF_GATE], ((0, 0), (0, 128 - (D_IDX + H_IDX))))
    return {"all": w_all, "kiwi": w_kiwi, "gate": w_all[:, OFF_GATE:]}


def _attn_project(h, wts, b, t):
    w_all = wts["all"]
    q = _matmul(h, w_all, bf16, scale=SM_SCALE_LOG2, cols=(0, Q_COLS), tn=WIDE_TN).reshape(b, t, -1)
    k = _matmul(h, w_all, f32, cols=(OFF_K, KV_COLS)).reshape(b, t, N_KV, HEAD_DIM)
    v = _matmul(h, w_all, f32, cols=(OFF_V, KV_COLS)).reshape(b, t, N_KV, HEAD_DIM)
    qi_hm = _matmul_heads(h, w_all, (OFF_QI, QI_COLS), D_IDX, bf16)
    kiwi = _matmul(h, wts["kiwi"], f32).reshape(b, t, 128)
    gate = _matmul(h, wts["gate"], bf16, tn=WIDE_TN).reshape(b, t, -1)
    ki = kiwi[..., :D_IDX]
    wi = kiwi[..., D_IDX:D_IDX + H_IDX]
    return q, k, v, qi_hm, ki, wi, gate


def _pad_rows(x, rows):
    return jnp.pad(x, ((0, 0), (0, rows - x.shape[1])) + ((0, 0),) * (x.ndim - 2))


def kernel(x_prompt, x_sample, cache_k, cache_v, cache_kidx, state_pool, norm_w, final_norm_w,
           attn_w_in, attn_w_out, rel_bias, pool_w_in, pool_group_w, pool_scale, pool_w_out):
    bp, sp, d = x_prompt.shape
    bs, ts, _ = x_sample.shape
    past = cache_k.shape[2]
    assert ATTN_TQ == ATTN_TK and ATTN_TK >= MAX_DISTANCE and SAMPLE_TK % ATTN_TK == 0
    assert sp % ATTN_TQ == 0 and past % SAMPLE_TK == 0 and past % SELECT_TK == 0
    assert past % CHUNK == 0 and ts <= CHUNK and ts % 16 == 0 and ts >= POOL_HIST
    depth = norm_w.shape[0]

    xp = x_prompt.reshape(bp * sp, d)
    xs = x_sample.reshape(bs * ts, d)
    tab_km = _bias_tables(rel_bias, True, ATTN_TQ)
    tab_qm = _bias_tables(rel_bias, False, ts)
    outs = {name: [] for name in ("kp", "vp", "kip", "poolp", "ks", "vs", "kis", "pools")}

    for layer in range(depth):
        hp = _rmsnorm(xp, norm_w[layer], bf16)
        hs = _rmsnorm(xs, norm_w[layer], bf16)
        if layer % 2 == 0:
            a = layer // 2
            wts = _attn_weights(attn_w_in[a])
            w_out = attn_w_out[a].astype(bf16)

            q, k, v, qi_hm, ki, wi, gate = _attn_project(hp, wts, bp, sp)
            nt = sp // ATTN_TK
            k_hm = jnp.transpose(k.astype(bf16), (0, 2, 1, 3)).reshape(bp, N_KV, nt, ATTN_TK, HEAD_DIM)
            vt_hm = jnp.transpose(v.astype(bf16).reshape(bp, nt, ATTN_TK, N_KV, HEAD_DIM), (0, 3, 1, 4, 2))
            vt_hm = jnp.concatenate([vt_hm, jnp.ones((bp, N_KV, nt, ONES_ROWS, ATTN_TK), bf16)], axis=3)
            og = _attn_prompt(tab_km, qi_hm, jnp.transpose(wi, (0, 2, 1)), q, gate,
                              ki.astype(bf16).reshape(bp, nt, ATTN_TK, D_IDX), k_hm, vt_hm)
            xp = _matmul(og.reshape(bp * sp, -1), w_out, f32, res=xp)
            outs["kp"].append(k); outs["vp"].append(v); outs["kip"].append(ki)

            q, k, v, qi_hm, ki, wi, gate = _attn_project(hs, wts, bs, ts)
            ckit = jnp.transpose(cache_kidx[a].astype(bf16), (0, 2, 1))
            nkit = jnp.transpose(_pad_rows(ki.astype(bf16), ATTN_TK), (0, 2, 1))
            nk = _pad_rows(k.astype(bf16).reshape(bs, ts, -1), ATTN_TK)
            nv = _pad_rows(v.astype(bf16).reshape(bs, ts, -1), ATTN_TK)
            mb = _select_sample(qi_hm, wi, ckit, nkit, ts)
            og = _attn_sample(tab_qm, mb, q, gate, cache_k[a].reshape(bs, past * N_KV, HEAD_DIM),
                              cache_v[a].reshape(bs, past * N_KV, HEAD_DIM), nk, nv)
            xs = _matmul(og.reshape(bs * ts, -1), w_out, f32, res=xs)
            outs["ks"].append(k); outs["vs"].append(v); outs["kis"].append(ki)
        else:
            p = layer // 2
            e = pool_w_in.shape[2] // 2
            w_in = pool_w_in[p].astype(bf16)
            gw = pool_group_w[p].astype(bf16)
            scale = pool_scale[p].reshape(1, e)
            w_out = pool_w_out[p].astype(bf16)

            u = _matmul(hp, w_in, f32, cols=(0, e), tn=WIDE_TN).reshape(bp, sp, e)
            gate = _matmul(hp, w_in, bf16, cols=(e, e), tn=WIDE_TN).reshape(bp, sp, e)
            zg = _pool_mix(u, jnp.zeros((bp, HALO, e), f32), gate, gw, scale, 0)
            xp = _matmul(zg.reshape(bp * sp, e), w_out, f32, res=xp)
            outs["poolp"].append(u[:, sp - POOL_HIST:])

            u = _matmul(hs, w_in, f32, cols=(0, e), tn=WIDE_TN).reshape(bs, ts, e)
            gate = _matmul(hs, w_in, bf16, cols=(e, e), tn=WIDE_TN).reshape(bs, ts, e)
            hist16 = jnp.pad(state_pool[p], ((0, 0), (HALO - POOL_HIST, 0), (0, 0)))
            zg = _pool_mix(u, hist16, gate, gw, scale, past)
            xs = _matmul(zg.reshape(bs * ts, e), w_out, f32, res=xs)
            outs["pools"].append(u[:, ts - POOL_HIST:])

    y_prompt = _rmsnorm(xp, final_norm_w, f32).reshape(bp, sp, d)
    y_sample = _rmsnorm(xs, final_norm_w, f32).reshape(bs, ts, d)
    return (y_prompt, y_sample, jnp.stack(outs["kp"]), jnp.stack(outs["vp"]), jnp.stack(outs["kip"]),
            jnp.stack(outs["poolp"]), jnp.stack(outs["ks"]), jnp.stack(outs["vs"]), jnp.stack(outs["kis"]),
            jnp.stack(outs["pools"]))
```

```python
import functools
import math

import jax
import jax.numpy as jnp
from jax import lax
from jax.experimental import pallas as pl
from jax.experimental.pallas import tpu as pltpu

CHUNK = 64
N_HEADS = 32
HEAD_DIM = 128
N_KV = 8
GROUP = N_HEADS // N_KV
H_IDX = 32
D_IDX = 64
TOPK_MAX = 256
NUM_BUCKETS = 32
MAX_DISTANCE = 128
FAR_BUCKET = NUM_BUCKETS // 2 - 1
POOL_WINDOWS = (2, 4, 8, 16)
N_POOL_GROUPS = len(POOL_WINDOWS)
POOL_HIST = max(POOL_WINDOWS) - 1
EPS = 1e-6

ATTN_TQ = 256
ATTN_TK = 256
SAMPLE_TK = 1024
WIDE_TN = 1024
SELECT_TK = 1024
ONES_ROWS = 16
HALO = 16
VMEM_LIMIT = 52 * 1024 * 1024

f32 = jnp.float32
bf16 = jnp.bfloat16
INT_MIN = -(2 ** 31)
NEG = -1e30
LOG2E = math.log2(math.e)
SM_SCALE_LOG2 = HEAD_DIM ** -0.5 * LOG2E


def _cparams(sem):
    return pltpu.CompilerParams(dimension_semantics=sem, vmem_limit_bytes=VMEM_LIMIT)


def _rmsnorm_kernel(x_ref, w_ref, o_ref):
    x = x_ref[...]
    y = x * lax.rsqrt(jnp.mean(x * x, axis=-1, keepdims=True) + EPS)
    o_ref[...] = (y * w_ref[...]).astype(o_ref.dtype)


def _rmsnorm(x, w, out_dtype, tm=512):
    m, d = x.shape
    tm = min(tm, m)
    assert m % tm == 0
    return pl.pallas_call(
        _rmsnorm_kernel,
        grid=(m // tm,),
        in_specs=[pl.BlockSpec((tm, d), lambda i: (i, 0)), pl.BlockSpec((1, d), lambda i: (0, 0))],
        out_specs=pl.BlockSpec((tm, d), lambda i: (i, 0)),
        out_shape=jax.ShapeDtypeStruct((m, d), out_dtype),
        compiler_params=_cparams(("parallel",)),
        name="rmsnorm",
    )(x, w.reshape(1, d))


def _mm_kernel(a_ref, w_ref, o_ref, *, scale):
    acc = jnp.dot(a_ref[...], w_ref[...], preferred_element_type=f32)
    if scale is not None:
        acc = acc * scale
    o_ref[...] = acc.astype(o_ref.dtype)


def _mm_res_kernel(a_ref, w_ref, r_ref, o_ref):
    acc = jnp.dot(a_ref[...], w_ref[...], preferred_element_type=f32)
    o_ref[...] = (r_ref[...] + acc).astype(o_ref.dtype)


def _matmul(a, w, out_dtype, res=None, scale=None, cols=None, tm=1024, tn=512):
    m, k = a.shape
    c0, n = (0, w.shape[1]) if cols is None else cols
    tm = min(tm, m)
    tn = min(tn, n)
    assert m % tm == 0 and n % tn == 0 and c0 % tn == 0
    j0 = c0 // tn
    in_specs = [pl.BlockSpec((tm, k), lambda i, j: (i, 0)), pl.BlockSpec((k, tn), lambda i, j: (0, j + j0))]
    args = [a, w]
    kern = functools.partial(_mm_kernel, scale=scale)
    if res is not None:
        assert scale is None
        in_specs.append(pl.BlockSpec((tm, tn), lambda i, j: (i, j)))
        args.append(res)
        kern = _mm_res_kernel
    return pl.pallas_call(
        kern,
        grid=(m // tm, n // tn),
        in_specs=in_specs,
        out_specs=pl.BlockSpec((tm, tn), lambda i, j: (i, j)),
        out_shape=jax.ShapeDtypeStruct((m, n), out_dtype),
        compiler_params=_cparams(("parallel", "parallel")),
        name="matmul_res" if res is not None else "matmul",
    )(*args)


def _mm_heads_kernel(a_ref, w_ref, o_ref):
    acc = jnp.dot(a_ref[...], w_ref[...], preferred_element_type=f32)
    dh = o_ref.shape[2]
    for hh in range(o_ref.shape[0]):
        o_ref[hh] = acc[:, hh * dh:(hh + 1) * dh].astype(o_ref.dtype)


def _matmul_heads(a, w, cols, dh, out_dtype, tm=1024, tn=512):
    m, k = a.shape
    c0, n = cols
    tm = min(tm, m)
    assert m % tm == 0 and n % tn == 0 and tn % dh == 0 and c0 % tn == 0
    j0 = c0 // tn
    return pl.pallas_call(
        _mm_heads_kernel,
        grid=(m // tm, n // tn),
        in_specs=[pl.BlockSpec((tm, k), lambda i, j: (i, 0)), pl.BlockSpec((k, tn), lambda i, j: (0, j + j0))],
        out_specs=pl.BlockSpec((tn // dh, tm, dh), lambda i, j: (j, i, 0)),
        out_shape=jax.ShapeDtypeStruct((n // dh, m, dh), out_dtype),
        compiler_params=_cparams(("parallel", "parallel")),
        name="matmul_heads",
    )(a, w)


def _rel_bucket(rel):
    nb = NUM_BUCKETS // 2
    ret = jnp.where(rel > 0, nb, 0)
    n = jnp.abs(rel)
    max_exact = nb // 2
    nf = jnp.maximum(n, 1).astype(f32)
    large = max_exact + (jnp.log(nf / max_exact) / math.log(MAX_DISTANCE / max_exact)
                         * (nb - max_exact)).astype(jnp.int32)
    large = jnp.minimum(large, nb - 1)
    return ret + jnp.where(n < max_exact, n, large)


def _bias_table_kernel(rb_ref, bk_ref, o_ref):
    bk = bk_ref[0]

    def head(h, carry):
        acc = jnp.zeros(bk.shape, f32)
        for b in range(NUM_BUCKETS):
            acc = jnp.where(bk == b, rb_ref[b, h], acc)
        o_ref[0, h] = (acc - rb_ref[FAR_BUCKET, h]) * LOG2E
        return carry

    lax.fori_loop(0, N_HEADS, head, 0)


def _bias_tables(rel_bias, key_major, n_queries):
    shape = (ATTN_TK, n_queries) if key_major else (n_queries, ATTN_TK)
    ax_q, ax_k = (1, 0) if key_major else (0, 1)
    qo = lax.broadcasted_iota(jnp.int32, shape, ax_q)
    ko = lax.broadcasted_iota(jnp.int32, shape, ax_k)
    buckets = jnp.stack([_rel_bucket(ko - qo), _rel_bucket(ko - ATTN_TK - qo)])
    return pl.pallas_call(
        _bias_table_kernel,
        grid=(2,),
        in_specs=[pl.BlockSpec(memory_space=pltpu.SMEM),
                  pl.BlockSpec((1,) + shape, lambda a: (a, 0, 0))],
        out_specs=pl.BlockSpec((1, N_HEADS) + shape, lambda a: (a, 0, 0, 0)),
        out_shape=jax.ShapeDtypeStruct((2, N_HEADS) + shape, f32),
        compiler_params=_cparams(("parallel",)),
        name="bias_tables",
    )(rel_bias, buckets)


def _float_order_key(x):
    bits = pltpu.bitcast(x + 0.0, jnp.int32)
    return bits ^ ((bits >> 31) & jnp.int32(0x7FFFFFFF))


def _score_keys_km(qi_ref, w_t, ki_t):
    tq = qi_ref.shape[1]
    heads_per_dot = 8
    acc = jnp.zeros((ki_t.shape[0], tq), f32)
    for hc in range(H_IDX // heads_per_dot):
        rhs = qi_ref[hc * heads_per_dot:(hc + 1) * heads_per_dot].reshape(heads_per_dot * tq, D_IDX)
        d = lax.dot_general(ki_t, rhs, (((1,), (1,)), ((), ())), preferred_element_type=f32)
        for hh in range(heads_per_dot):
            h = hc * heads_per_dot + hh
            acc = acc + jnp.maximum(d[:, hh * tq:(hh + 1) * tq], 0.0) * w_t[h:h + 1, :]
    return _float_order_key(acc)


def _kth_largest_key_km(key_scr, n_tiles, k_sel):
    tk, tq = key_scr.shape[1], key_scr.shape[2]
    empty_tile = key_scr.shape[0] - 1

    def bit_step(bi, tvec):
        cand = tvec + lax.shift_left(jnp.int32(1), 31 - bi)

        def count(t, c):
            one = jnp.where(key_scr[t] >= cand, 1.0, 0.0)
            return c + jnp.sum(one.reshape(tk // 8, 8, tq), axis=0)

        def count_pair(tau, c):
            t1 = 2 * tau + 1
            return count(jnp.where(t1 < n_tiles, t1, empty_tile), count(2 * tau, c))

        c = lax.fori_loop(0, (n_tiles + 1) // 2, count_pair, jnp.zeros((8, tq), f32))
        cnt = jnp.sum(c, axis=0, keepdims=True)
        return jnp.where(cnt >= float(k_sel), cand, tvec)

    return lax.fori_loop(0, 32, bit_step, jnp.full((1, tq), INT_MIN, jnp.int32))


def _softmax_stage(s_ref, p_ref, mask_bias, m_scr, acc_scr):
    lg = s_ref[...] + mask_bias
    m_prev = m_scr[...]
    m_new = jnp.maximum(m_prev, jnp.max(lg, axis=0, keepdims=True))
    p_ref[...] = jnp.exp2(lg - m_new).astype(bf16)
    acc_scr[...] = acc_scr[...] * jnp.exp2(m_prev - m_new)
    m_scr[...] = m_new


def _attn_prompt_kernel(qi_ref, wi_ref, q_ref, gate_ref, ki_ref, k_ref, vt_ref, tab_ref, o_ref,
                        key_scr, mb_scr, sa_scr, sb_scr, pa_scr, pb_scr, m_scr, acc_scr, *, k_sel):
    i = pl.program_id(1)
    g = pl.program_id(2)
    tq, tk = ATTN_TQ, ATTN_TK
    masked_tile = mb_scr.shape[0] - 1
    ko = lax.broadcasted_iota(jnp.int32, (tk, tq), 0)
    qo = lax.broadcasted_iota(jnp.int32, (tk, tq), 1)
    causal = (ko // CHUNK) <= (qo // CHUNK)

    @pl.when(g == 0)
    def _select():
        w_t = wi_ref[0] * (D_IDX ** -0.5 * H_IDX ** -0.5)

        def score_tile(t, carry):
            key_scr[t] = _score_keys_km(qi_ref, w_t, ki_ref[0, t])
            return carry

        lax.fori_loop(0, i + 1, score_tile, 0)
        key_scr[i] = jnp.where(causal, key_scr[i], INT_MIN)
        key_scr[masked_tile] = jnp.full((tk, tq), INT_MIN, jnp.int32)
        mb_scr[masked_tile] = jnp.full((tk, tq), NEG, f32)
        tvec = _kth_largest_key_km(key_scr, i + 1, k_sel)

        def mask_tile(t, carry):
            mb_scr[t] = jnp.where(key_scr[t] >= tvec, 0.0, NEG)
            return carry

        lax.fori_loop(0, i + 1, mask_tile, 0)
        mb_scr[i] = jnp.where(causal, mb_scr[i], NEG)

    n_far = jnp.maximum(i - 1, 0)
    n_pairs = (n_far + 1) // 2
    t_prev = jnp.maximum(i - 1, 0)
    mask_prev = jnp.where(i >= 1, i - 1, masked_tile)
    m_scr[...] = jnp.full(m_scr.shape, NEG, f32)
    acc_scr[...] = jnp.zeros(acc_scr.shape, f32)
    qs = jnp.concatenate([q_ref[0, :, hh * HEAD_DIM:(hh + 1) * HEAD_DIM] for hh in range(GROUP)], axis=0)

    def qk(t):
        return lax.dot_general(k_ref[0, 0, t], qs, (((1,), (1,)), ((), ())), preferred_element_type=f32)

    def pv(t, p_ref):
        return jnp.dot(vt_ref[0, 0, t], p_ref[...], preferred_element_type=f32)

    def far_mask(t):
        return jnp.concatenate([mb_scr[jnp.where(t < n_far, t, masked_tile)]] * GROUP, axis=1)

    def near_mask(mask_t, a):
        mbt = mb_scr[mask_t]
        return jnp.concatenate([mbt + tab_ref[a, hh] for hh in range(GROUP)], axis=1)

    sa_scr[...] = qk(0)

    def pair_step(tau, carry):
        t0 = 2 * tau
        t1 = jnp.minimum(t0 + 1, n_far - 1)
        sb_scr[...] = qk(t1)
        _softmax_stage(sa_scr, pa_scr, far_mask(t0), m_scr, acc_scr)
        acc_scr[...] += pv(t0, pa_scr)
        sa_scr[...] = qk(jnp.minimum(t0 + 2, n_far))
        _softmax_stage(sb_scr, pb_scr, far_mask(t0 + 1), m_scr, acc_scr)
        acc_scr[...] += pv(t1, pb_scr)
        return carry

    lax.fori_loop(0, n_pairs, pair_step, 0)
    sb_scr[...] = qk(i)
    _softmax_stage(sa_scr, pa_scr, near_mask(mask_prev, 1), m_scr, acc_scr)
    acc_scr[...] += pv(t_prev, pa_scr)
    _softmax_stage(sb_scr, pb_scr, near_mask(i, 0), m_scr, acc_scr)
    acc = acc_scr[...] + pv(i, pb_scr)
    o_t = acc[0:HEAD_DIM] / acc[HEAD_DIM:HEAD_DIM + 1]
    for hh in range(GROUP):
        o = o_t[:, hh * tq:(hh + 1) * tq].T
        gt = gate_ref[0, :, hh * HEAD_DIM:(hh + 1) * HEAD_DIM].astype(f32)
        o_ref[0, :, hh * HEAD_DIM:(hh + 1) * HEAD_DIM] = (o * (gt * jax.nn.sigmoid(gt))).astype(o_ref.dtype)


def _attn_prompt(tab, qi_hm, wi_t, q, gate, ki, k_hm, vt_hm):
    b, s, width = q.shape
    tq, tk = ATTN_TQ, ATTN_TK
    nt = s // tk
    nq = s // tq
    k_sel = min(TOPK_MAX, s // 4)
    gw = GROUP * HEAD_DIM
    vrows = HEAD_DIM + ONES_ROWS
    return pl.pallas_call(
        functools.partial(_attn_prompt_kernel, k_sel=k_sel),
        grid=(b, nq, N_KV),
        in_specs=[
            pl.BlockSpec((H_IDX, tq, D_IDX), lambda bb, i, g: (0, bb * nq + i, 0)),
            pl.BlockSpec((1, H_IDX, tq), lambda bb, i, g: (bb, 0, i)),
            pl.BlockSpec((1, tq, gw), lambda bb, i, g: (bb, i, g)),
            pl.BlockSpec((1, tq, gw), lambda bb, i, g: (bb, i, g)),
            pl.BlockSpec((1, nt, tk, D_IDX), lambda bb, i, g: (bb, 0, 0, 0)),
            pl.BlockSpec((1, 1, nt, tk, HEAD_DIM), lambda bb, i, g: (bb, g, 0, 0, 0)),
            pl.BlockSpec((1, 1, nt, vrows, tk), lambda bb, i, g: (bb, g, 0, 0, 0)),
            pl.BlockSpec((2, GROUP, tk, tq), lambda bb, i, g: (0, g, 0, 0)),
        ],
        out_specs=pl.BlockSpec((1, tq, gw), lambda bb, i, g: (bb, i, g)),
        out_shape=jax.ShapeDtypeStruct((b, s, width), bf16),
        scratch_shapes=[
            pltpu.VMEM((nt + 1, tk, tq), jnp.int32),
            pltpu.VMEM((nt + 1, tk, tq), f32),
            pltpu.VMEM((tk, GROUP * tq), f32),
            pltpu.VMEM((tk, GROUP * tq), f32),
            pltpu.VMEM((tk, GROUP * tq), bf16),
            pltpu.VMEM((tk, GROUP * tq), bf16),
            pltpu.VMEM((1, GROUP * tq), f32),
            pltpu.VMEM((vrows, GROUP * tq), f32),
        ],
        compiler_params=_cparams(("parallel", "arbitrary", "arbitrary")),
        name="attn_prompt",
    )(qi_hm, wi_t, q, gate, ki, k_hm, vt_hm, tab)


def _score_keys_qm(qi_ref, w, kt):
    tq = qi_ref.shape[1]
    heads_per_dot = 8
    acc = jnp.zeros((tq, kt.shape[1]), f32)
    for hc in range(H_IDX // heads_per_dot):
        lhs = qi_ref[hc * heads_per_dot:(hc + 1) * heads_per_dot].reshape(heads_per_dot * tq, D_IDX)
        d = jnp.dot(lhs, kt, preferred_element_type=f32)
        for hh in range(heads_per_dot):
            h = hc * heads_per_dot + hh
            acc = acc + jnp.maximum(d[hh * tq:(hh + 1) * tq], 0.0) * w[:, h:h + 1]
    return _float_order_key(acc)


def _kth_largest_key_qm(key_scr, k_sel):
    tq, width = key_scr.shape

    def bit_step(bi, tvec):
        cand = tvec + lax.shift_left(jnp.int32(1), 31 - bi)
        c = jnp.zeros((tq, 128), f32)
        for s in range(width // 128):
            c = c + jnp.where(key_scr[:, s * 128:(s + 1) * 128] >= cand, 1.0, 0.0)
        cnt = jnp.sum(c, axis=1, keepdims=True)
        return jnp.where(cnt >= float(k_sel), cand, tvec)

    return lax.fori_loop(0, 32, bit_step, jnp.full((tq, 1), INT_MIN, jnp.int32))


def _select_sample_kernel(qi_ref, wi_ref, ckit_ref, nkit_ref, mb_ref, key_scr, *, k_sel, n_new):
    tq = qi_ref.shape[1]
    tk = ATTN_TK
    p = ckit_ref.shape[2]
    real = lax.broadcasted_iota(jnp.int32, (tq, tk), 1) < n_new
    w = wi_ref[0] * (D_IDX ** -0.5 * H_IDX ** -0.5)
    for c in range(p // SELECT_TK):
        cols = slice(c * SELECT_TK, (c + 1) * SELECT_TK)
        key_scr[:, cols] = _score_keys_qm(qi_ref, w, ckit_ref[0, :, cols])
    key_scr[:, p:p + tk] = jnp.where(real, _score_keys_qm(qi_ref, w, nkit_ref[0]), INT_MIN)
    tvec = _kth_largest_key_qm(key_scr, k_sel)
    mb_ref[0, :, 0:p] = jnp.where(key_scr[:, 0:p] >= tvec, 0.0, NEG)
    mb_ref[0, :, p:p + tk] = jnp.where(real & (key_scr[:, p:p + tk] >= tvec), 0.0, NEG)


def _select_sample(qi_hm, wi, ckit, nkit, n_new):
    b, t, _ = wi.shape
    p = ckit.shape[2]
    tk = ATTN_TK
    k_sel = min(TOPK_MAX, (p + n_new) // 4)
    return pl.pallas_call(
        functools.partial(_select_sample_kernel, k_sel=k_sel, n_new=n_new),
        grid=(b,),
        in_specs=[
            pl.BlockSpec((H_IDX, t, D_IDX), lambda bb: (0, bb, 0)),
            pl.BlockSpec((1, t, H_IDX), lambda bb: (bb, 0, 0)),
            pl.BlockSpec((1, D_IDX, p), lambda bb: (bb, 0, 0)),
            pl.BlockSpec((1, D_IDX, tk), lambda bb: (bb, 0, 0)),
        ],
        out_specs=pl.BlockSpec((1, t, p + tk), lambda bb: (bb, 0, 0)),
        out_shape=jax.ShapeDtypeStruct((b, t, p + tk), f32),
        scratch_shapes=[pltpu.VMEM((t, p + tk), jnp.int32)],
        compiler_params=_cparams(("parallel",)),
        name="select_sample",
    )(qi_hm, wi, ckit, nkit)


def _attend_tile_qm(q_ref, k_of, v_of, mbt, bias_of, s_scr, p_scr, m_scr, l_scr, acc_scr):
    w = mbt.shape[1]
    mb4 = jnp.concatenate([mbt] * GROUP, axis=0)
    for g in range(N_KV):
        qs = jnp.concatenate(
            [q_ref[0, :, (g * GROUP + hh) * HEAD_DIM:(g * GROUP + hh + 1) * HEAD_DIM] for hh in range(GROUP)], axis=0)
        s_scr[g, :, 0:w] = lax.dot_general(qs, k_of(g), (((1,), (1,)), ((), ())), preferred_element_type=f32)
    for g in range(N_KV):
        lg = s_scr[g, :, 0:w] + mb4
        if bias_of is not None:
            lg = lg + bias_of(g)
        m_prev = m_scr[g]
        m_new = jnp.maximum(m_prev, jnp.max(lg, axis=1, keepdims=True))
        alpha = jnp.exp2(m_prev - m_new)
        p = jnp.exp2(lg - m_new)
        l_scr[g] = alpha * l_scr[g] + jnp.sum(p, axis=1, keepdims=True)
        acc_scr[g] = alpha * acc_scr[g]
        m_scr[g] = m_new
        p_scr[g, :, 0:w] = p.astype(bf16)
    for g in range(N_KV):
        acc_scr[g] = acc_scr[g] + jnp.dot(p_scr[g, :, 0:w], v_of(g), preferred_element_type=f32)


def _attn_sample_kernel(q_ref, gate_ref, ck_ref, cv_ref, nk_ref, nv_ref, mbc_ref, mbn_ref, tab_ref, o_ref,
                        s_scr, p_scr, m_scr, l_scr, acc_scr):
    j = pl.program_id(1)
    nj = pl.num_programs(1)
    t = q_ref.shape[1]
    tk = ATTN_TK
    n_keys = mbc_ref.shape[2]

    @pl.when(j == 0)
    def _init():
        m_scr[...] = jnp.full(m_scr.shape, NEG, f32)
        l_scr[...] = jnp.zeros(l_scr.shape, f32)
        acc_scr[...] = jnp.zeros(acc_scr.shape, f32)

    def head_cols(g):
        return slice(g * HEAD_DIM, (g + 1) * HEAD_DIM)

    def cache_head(ref, g):
        return ref[0, pl.ds(g, n_keys, stride=N_KV), :].astype(bf16)

    def cache_tile(last):
        bias_of = None
        if last:
            far = jnp.zeros((t, n_keys - tk), f32)

            def bias_of(g):
                return jnp.concatenate(
                    [jnp.concatenate([far, tab_ref[1, g * GROUP + hh]], axis=1) for hh in range(GROUP)], axis=0)

        _attend_tile_qm(q_ref, functools.partial(cache_head, ck_ref), functools.partial(cache_head, cv_ref),
                        mbc_ref[0], bias_of, s_scr, p_scr, m_scr, l_scr, acc_scr)

    @pl.when(j < nj - 1)
    def _far():
        cache_tile(False)

    @pl.when(j == nj - 1)
    def _near():
        cache_tile(True)
        _attend_tile_qm(q_ref, lambda g: nk_ref[0, :, head_cols(g)], lambda g: nv_ref[0, :, head_cols(g)],
                        mbn_ref[0],
                        lambda g: jnp.concatenate([tab_ref[0, g * GROUP + hh] for hh in range(GROUP)], axis=0),
                        s_scr, p_scr, m_scr, l_scr, acc_scr)
        for g in range(N_KV):
            o = acc_scr[g] / l_scr[g]
            for hh in range(GROUP):
                hcols = head_cols(g * GROUP + hh)
                gt = gate_ref[0, :, hcols].astype(f32)
                o_ref[0, :, hcols] = (o[hh * t:(hh + 1) * t] * (gt * jax.nn.sigmoid(gt))).astype(o_ref.dtype)


def _attn_sample(tab, mb, q, gate, ck, cv, nk, nv):
    b, t, width = q.shape
    tk = ATTN_TK
    p = ck.shape[1] // N_KV
    nj = p // SAMPLE_TK
    kvw = N_KV * HEAD_DIM
    return pl.pallas_call(
        _attn_sample_kernel,
        grid=(b, nj),
        in_specs=[
            pl.BlockSpec((1, t, width), lambda bb, j: (bb, 0, 0)),
            pl.BlockSpec((1, t, width), lambda bb, j: (bb, 0, 0)),
            pl.BlockSpec((1, SAMPLE_TK * N_KV, HEAD_DIM), lambda bb, j: (bb, j, 0)),
            pl.BlockSpec((1, SAMPLE_TK * N_KV, HEAD_DIM), lambda bb, j: (bb, j, 0)),
            pl.BlockSpec((1, tk, kvw), lambda bb, j: (bb, 0, 0)),
            pl.BlockSpec((1, tk, kvw), lambda bb, j: (bb, 0, 0)),
            pl.BlockSpec((1, t, SAMPLE_TK), lambda bb, j: (bb, 0, j)),
            pl.BlockSpec((1, t, tk), lambda bb, j: (bb, 0, p // tk)),
            pl.BlockSpec((2, N_HEADS, t, tk), lambda bb, j: (0, 0, 0, 0)),
        ],
        out_specs=pl.BlockSpec((1, t, width), lambda bb, j: (bb, 0, 0)),
        out_shape=jax.ShapeDtypeStruct((b, t, width), bf16),
        scratch_shapes=[
            pltpu.VMEM((N_KV, GROUP * t, SAMPLE_TK), f32),
            pltpu.VMEM((N_KV, GROUP * t, SAMPLE_TK), bf16),
            pltpu.VMEM((N_KV, GROUP * t, 1), f32),
            pltpu.VMEM((N_KV, GROUP * t, 1), f32),
            pltpu.VMEM((N_KV, GROUP * t, HEAD_DIM), f32),
        ],
        compiler_params=_cparams(("parallel", "arbitrary")),
        name="attn_sample",
    )(q, gate, ck, cv, nk, nv, mb, mb, tab)


def _pool_kernel(u_ref, hist_ref, gate_ref, gw_ref, scale_ref, o_ref, ext_scr, *, start):
    i = pl.program_id(1)
    tm = u_ref.shape[1]
    gwid = gw_ref.shape[1]

    @pl.when(i == 0)
    def _first():
        ext_scr[0:HALO, :] = hist_ref[0]

    @pl.when(i > 0)
    def _carry():
        ext_scr[0:HALO, :] = ext_scr[tm:tm + HALO, :]

    ext_scr[HALO:HALO + tm, :] = u_ref[0]
    pos = start + i * tm + lax.broadcasted_iota(jnp.int32, (tm, 1), 0)
    for gi, win in enumerate(POOL_WINDOWS):
        lo, hi = gi * gwid, (gi + 1) * gwid
        s = ext_scr[:, lo:hi]
        shift = 1
        while shift < win:
            s = s + pltpu.roll(s, shift, 0)
            shift *= 2
        s = s[HALO:HALO + tm]
        tok = ext_scr[HALO:HALO + tm, lo:hi]
        cnt = jnp.minimum(pos + 1, win).astype(f32)
        mix = (s / cnt - tok).astype(bf16)
        z = jnp.dot(mix, gw_ref[gi], preferred_element_type=f32) * scale_ref[:, lo:hi]
        gt = gate_ref[0, :, lo:hi].astype(f32)
        o_ref[0, :, lo:hi] = (z * (gt * jax.nn.sigmoid(gt))).astype(o_ref.dtype)


def _pool_mix(u, hist16, gate, group_w, scale, start, tm=256):
    b, t, e = u.shape
    tm = min(tm, t)
    gwid = e // N_POOL_GROUPS
    return pl.pallas_call(
        functools.partial(_pool_kernel, start=start),
        grid=(b, t // tm),
        in_specs=[
            pl.BlockSpec((1, tm, e), lambda bb, i: (bb, i, 0)),
            pl.BlockSpec((1, HALO, e), lambda bb, i: (bb, 0, 0)),
            pl.BlockSpec((1, tm, e), lambda bb, i: (bb, i, 0)),
            pl.BlockSpec((N_POOL_GROUPS, gwid, gwid), lambda bb, i: (0, 0, 0)),
            pl.BlockSpec((1, e), lambda bb, i: (0, 0)),
        ],
        out_specs=pl.BlockSpec((1, tm, e), lambda bb, i: (bb, i, 0)),
        out_shape=jax.ShapeDtypeStruct((b, t, e), bf16),
        scratch_shapes=[pltpu.VMEM((HALO + tm, e), f32)],
        compiler_params=_cparams(("parallel", "arbitrary")),
        name="pool_mix",
    )(u, hist16, gate, group_w, scale)


Q_COLS = N_HEADS * HEAD_DIM
KV_COLS = N_KV * HEAD_DIM
QI_COLS = H_IDX * D_IDX
OFF_K = Q_COLS
OFF_V = OFF_K + KV_COLS
OFF_QI = OFF_V + KV_COLS
OFF_KIWI = OFF_QI + QI_COLS
OFF_GATE = OFF_KIWI + D_IDX + H_IDX


def _attn_weights(w_in):
    w_all = w_in.astype(bf16)
    w_kiwi = jnp.pad(w_all[:, OFF_KIWI:OFF_GATE], ((0, 0), (0, 128 - (D_IDX + H_IDX))))
    return {"all": w_all, "kiwi": w_kiwi, "gate": w_all[:, OFF_GATE:]}


def _attn_project(h, wts, b, t):
    w_all = wts["all"]
    q = _matmul(h, w_all, bf16, scale=SM_SCALE_LOG2, cols=(0, Q_COLS), tn=WIDE_TN).reshape(b, t, -1)
    k = _matmul(h, w_all, f32, cols=(OFF_K, KV_COLS)).reshape(b, t, N_KV, HEAD_DIM)
    v = _matmul(h, w_all, f32, cols=(OFF_V, KV_COLS)).reshape(b, t, N_KV, HEAD_DIM)
    qi_hm = _matmul_heads(h, w_all, (OFF_QI, QI_COLS), D_IDX, bf16)
    kiwi = _matmul(h, wts["kiwi"], f32).reshape(b, t, 128)
    gate = _matmul(h, wts["gate"], bf16, tn=WIDE_TN).reshape(b, t, -1)
    ki = kiwi[..., :D_IDX]
    wi = kiwi[..., D_IDX:D_IDX + H_IDX]
    return q, k, v, qi_hm, ki, wi, gate


def _pad_rows(x, rows):
    return jnp.pad(x, ((0, 0), (0, rows - x.shape[1])) + ((0, 0),) * (x.ndim - 2))


def kernel(x_prompt, x_sample, cache_k, cache_v, cache_kidx, state_pool, norm_w, final_norm_w,
           attn_w_in, attn_w_out, rel_bias, pool_w_in, pool_group_w, pool_scale, pool_w_out):
    bp, sp, d = x_prompt.shape
    bs, ts, _ = x_sample.shape
    past = cache_k.shape[2]
    assert ATTN_TQ == ATTN_TK and ATTN_TK >= MAX_DISTANCE and SAMPLE_TK % ATTN_TK == 0
    assert sp % ATTN_TQ == 0 and past % SAMPLE_TK == 0 and past % SELECT_TK == 0
    assert past % CHUNK == 0 and ts <= CHUNK and ts % 16 == 0 and ts >= POOL_HIST
    depth = norm_w.shape[0]

    xp = x_prompt.reshape(bp * sp, d)
    xs = x_sample.reshape(bs * ts, d)
    tab_km = _bias_tables(rel_bias, True, ATTN_TQ)
    tab_qm = _bias_tables(rel_bias, False, ts)
    outs = {name: [] for name in ("kp", "vp", "kip", "poolp", "ks", "vs", "kis", "pools")}

    for layer in range(depth):
        hp = _rmsnorm(xp, norm_w[layer], bf16)
        hs = _rmsnorm(xs, norm_w[layer], bf16)
        if layer % 2 == 0:
            a = layer // 2
            wts = _attn_weights(attn_w_in[a])
            w_out = attn_w_out[a].astype(bf16)

            q, k, v, qi_hm, ki, wi, gate = _attn_project(hp, wts, bp, sp)
            nt = sp // ATTN_TK
            k_hm = jnp.transpose(k.astype(bf16), (0, 2, 1, 3)).reshape(bp, N_KV, nt, ATTN_TK, HEAD_DIM)
            vt_hm = jnp.transpose(v.astype(bf16).reshape(bp, nt, ATTN_TK, N_KV, HEAD_DIM), (0, 3, 1, 4, 2))
            vt_hm = jnp.concatenate([vt_hm, jnp.ones((bp, N_KV, nt, ONES_ROWS, ATTN_TK), bf16)], axis=3)
            og = _attn_prompt(tab_km, qi_hm, jnp.transpose(wi, (0, 2, 1)), q, gate,
                              ki.astype(bf16).reshape(bp, nt, ATTN_TK, D_IDX), k_hm, vt_hm)
            xp = _matmul(og.reshape(bp * sp, -1), w_out, f32, res=xp)
            outs["kp"].append(k); outs["vp"].append(v); outs["kip"].append(ki)

            q, k, v, qi_hm, ki, wi, gate = _attn_project(hs, wts, bs, ts)
            ckit = jnp.transpose(cache_kidx[a].astype(bf16), (0, 2, 1))
            nkit = jnp.transpose(_pad_rows(ki.astype(bf16), ATTN_TK), (0, 2, 1))
            nk = _pad_rows(k.astype(bf16).reshape(bs, ts, -1), ATTN_TK)
            nv = _pad_rows(v.astype(bf16).reshape(bs, ts, -1), ATTN_TK)
            mb = _select_sample(qi_hm, wi, ckit, nkit, ts)
            og = _attn_sample(tab_qm, mb, q, gate, cache_k[a].reshape(bs, past * N_KV, HEAD_DIM),
                              cache_v[a].reshape(bs, past * N_KV, HEAD_DIM), nk, nv)
            xs = _matmul(og.reshape(bs * ts, -1), w_out, f32, res=xs)
            outs["ks"].append(k); outs["vs"].append(v); outs["kis"].append(ki)
        else:
            p = layer // 2
            e = pool_w_in.shape[2] // 2
            w_in = pool_w_in[p].astype(bf16)
            gw = pool_group_w[p].astype(bf16)
            scale = pool_scale[p].reshape(1, e)
            w_out = pool_w_out[p].astype(bf16)

            u = _matmul(hp, w_in, f32, cols=(0, e), tn=WIDE_TN).reshape(bp, sp, e)
            gate = _matmul(hp, w_in, bf16, cols=(e, e), tn=WIDE_TN).reshape(bp, sp, e)
            zg = _pool_mix(u, jnp.zeros((bp, HALO, e), f32), gate, gw, scale, 0)
            xp = _matmul(zg.reshape(bp * sp, e), w_out, f32, res=xp)
            outs["poolp"].append(u[:, sp - POOL_HIST:])

            u = _matmul(hs, w_in, f32, cols=(0, e), tn=WIDE_TN).reshape(bs, ts, e)
            gate = _matmul(hs, w_in, bf16, cols=(e, e), tn=WIDE_TN).reshape(bs, ts, e)
            hist16 = jnp.pad(state_pool[p], ((0, 0), (HALO - POOL_HIST, 0), (0, 0)))
            zg = _pool_mix(u, hist16, gate, gw, scale, past)
            xs = _matmul(zg.reshape(bs * ts, e), w_out, f32, res=xs)
            outs["pools"].append(u[:, ts - POOL_HIST:])

    y_prompt = _rmsnorm(xp, final_norm_w, f32).reshape(bp, sp, d)
    y_sample = _rmsnorm(xs, final_norm_w, f32).reshape(bs, ts, d)
    return (y_prompt, y_sample, jnp.stack(outs["kp"]), jnp.stack(outs["vp"]), jnp.stack(outs["kip"]),
            jnp.stack(outs["poolp"]), jnp.stack(outs["ks"]), jnp.stack(outs["vs"]), jnp.stack(outs["kis"]),
            jnp.stack(outs["pools"]))
```
